```python
import math
import jax, jax.numpy as jnp
from jax import lax
import numpy as np

D_MODEL = 1024
BATCH = 8
SEQ = 8192
DEPTH = 1
DEC_BATCH = 1
DEC_SEQ = 16384
PAST_LEN = 128

MIX_WIDTH = D_MODEL
POOL_WIDTH = 3 * D_MODEL // 8
POOL_WINDOWS = (2, 4, 8, 16)
N_POOL_GROUPS = len(POOL_WINDOWS)
POOL_GROUP = POOL_WIDTH // N_POOL_GROUPS
CONV_WIDTH = 3 * D_MODEL // 8
CONV_KERNEL = 31
ATTN_WIDTH = MIX_WIDTH - POOL_WIDTH - CONV_WIDTH
N_XATTN_HEADS = 4
XATTN_HEAD_DIM = ATTN_WIDTH // N_XATTN_HEADS
XATTN_SCALE = 1.0 / math.sqrt(XATTN_HEAD_DIM)
N_MEM = 256
D_FF = ((8 * D_MODEL // 3 + 255) // 256) * 256
IN_WIDTH = POOL_WIDTH + 2 * CONV_WIDTH + ATTN_WIDTH
LN_EPS = 1e-5
DEEPNORM_ALPHA = (2.0 * DEPTH) ** 0.25
DEEPNORM_BETA = (8.0 * DEPTH) ** -0.25
MACARON_WEIGHT = 0.5

kernel_name = 'hybrid_pool_conv_memory_encoder'


def layer_norm(x, g, b):
    xf = x.astype(jnp.float32)
    mu = jnp.mean(xf, axis=-1, keepdims=True)
    xc = xf - mu
    var = jnp.mean(xc * xc, axis=-1, keepdims=True)
    y = xc * lax.rsqrt(var + LN_EPS)
    return (y * g.astype(jnp.float32) + b.astype(jnp.float32)).astype(x.dtype)


def swiglu(x, w_gate, w_up, w_down):
    return (jax.nn.silu(x @ w_gate) * (x @ w_up)) @ w_down


def pool_mixer(u, pool_w, pool_scale):
    B, S, C = u.shape
    uf = u.astype(jnp.float32)
    cs = jnp.concatenate([jnp.zeros((B, 1, C), jnp.float32), jnp.cumsum(uf, axis=1)], axis=1)
    t = jnp.arange(S)
    outs = []
    for g, w in enumerate(POOL_WINDOWS):
        left = w // 2
        right = w - 1 - left
        lo = jnp.clip(t - left, 0, S)
        hi = jnp.clip(t + right + 1, 0, S)
        c0, c1 = g * POOL_GROUP, (g + 1) * POOL_GROUP
        csg = cs[..., c0:c1]
        win_sum = jnp.take(csg, hi, axis=1) - jnp.take(csg, lo, axis=1)
        win_mean = win_sum / (hi - lo).astype(jnp.float32)[None, :, None]
        outs.append(win_mean - uf[..., c0:c1])
    m = jnp.stack(outs, axis=2).astype(u.dtype)
    y = jnp.einsum('bsgc,gcd->bsgd', m, pool_w).reshape(B, S, POOL_WIDTH)
    return y * pool_scale


def conv_module(a, gate, dw_w, dw_b, ln_g, ln_b):
    h = a * jax.nn.sigmoid(gate)
    pad = CONV_KERNEL // 2
    h = lax.conv_general_dilated(h, dw_w[:, None, :].astype(h.dtype), window_strides=(1,), padding=[(pad, pad)],
                                 dimension_numbers=('NWC', 'WIO', 'NWC'), feature_group_count=CONV_WIDTH) + dw_b
    h = layer_norm(h, ln_g, ln_b)
    return jax.nn.silu(h)


def memory_cross_attention(q, mem, mem_ln_g, mem_ln_b, w_kv):
    B, S, _ = q.shape
    memn = layer_norm(mem, mem_ln_g, mem_ln_b)
    kv = memn @ w_kv
    k = kv[..., :ATTN_WIDTH].reshape(B, -1, N_XATTN_HEADS, XATTN_HEAD_DIM)
    v = kv[..., ATTN_WIDTH:].reshape(B, -1, N_XATTN_HEADS, XATTN_HEAD_DIM)
    qh = q.reshape(B, S, N_XATTN_HEADS, XATTN_HEAD_DIM)
    scores = jnp.einsum('bshd,bmhd->bhsm', qh.astype(jnp.float32), k.astype(jnp.float32)) * XATTN_SCALE
    p = jax.nn.softmax(scores, axis=-1)
    o = jnp.einsum('bhsm,bmhd->bshd', p.astype(v.dtype), v)
    return o.reshape(B, S, ATTN_WIDTH)


def encoder_layer(x, mem,
                  ffn1_w_gate, ffn1_w_up, ffn1_w_down, ln1_g, ln1_b,
                  w_in, pool_w, pool_scale, conv_dw_w, conv_dw_b, conv_ln_g, conv_ln_b,
                  mem_ln_g, mem_ln_b, w_kv, w_out, ln2_g, ln2_b,
                  ffn2_w_gate, ffn2_w_up, ffn2_w_down, ln3_g, ln3_b):
    x = layer_norm(DEEPNORM_ALPHA * x + MACARON_WEIGHT * swiglu(x, ffn1_w_gate, ffn1_w_up, ffn1_w_down), ln1_g, ln1_b)
    z = x @ w_in
    u = z[..., :POOL_WIDTH]
    a = z[..., POOL_WIDTH:POOL_WIDTH + CONV_WIDTH]
    gt = z[..., POOL_WIDTH + CONV_WIDTH:POOL_WIDTH + 2 * CONV_WIDTH]
    q = z[..., POOL_WIDTH + 2 * CONV_WIDTH:]
    y_pool = pool_mixer(u, pool_w, pool_scale)
    y_conv = conv_module(a, gt, conv_dw_w, conv_dw_b, conv_ln_g, conv_ln_b)
    y_attn = memory_cross_attention(q, mem, mem_ln_g, mem_ln_b, w_kv)
    mix = jnp.concatenate([y_pool, y_conv, y_attn], axis=-1) @ w_out
    x = layer_norm(DEEPNORM_ALPHA * x + mix, ln2_g, ln2_b)
    x = layer_norm(DEEPNORM_ALPHA * x + MACARON_WEIGHT * swiglu(x, ffn2_w_gate, ffn2_w_up, ffn2_w_down), ln3_g, ln3_b)
    return x


def _normal(k, shape, scale):
    return scale * jax.random.normal(k, shape, jnp.float32)


def setup_inputs(seed: int = 0) -> dict:
    key = jax.random.key(seed)
    ks = jax.random.split(key, 27)
    L, D = DEPTH, D_MODEL
    return {
        'x_prompt': _normal(ks[0], (BATCH, SEQ, D), 1.0),
        'x_sample': _normal(ks[1], (DEC_BATCH, DEC_SEQ, D), 1.0),
        'mem_prompt': _normal(ks[2], (BATCH, N_MEM, D), 1.0),
        'mem_sample': _normal(ks[3], (DEC_BATCH, N_MEM, D), 1.0),
        'ffn1_w_gate': _normal(ks[4], (L, D, D_FF), D ** -0.5),
        'ffn1_w_up': _normal(ks[5], (L, D, D_FF), D ** -0.5),
        'ffn1_w_down': _normal(ks[6], (L, D_FF, D), DEEPNORM_BETA * D_FF ** -0.5),
        'ln1_g': 1.0 + _normal(ks[7], (L, D), 0.02),
        'ln1_b': _normal(ks[8], (L, D), 0.02),
        'w_in': _normal(ks[9], (L, D, IN_WIDTH), D ** -0.5),
        'pool_w': _normal(ks[10], (L, N_POOL_GROUPS, POOL_GROUP, POOL_GROUP), POOL_GROUP ** -0.5),
        'pool_scale': 1.0 + _normal(ks[11], (L, POOL_WIDTH), 0.02),
        'conv_dw_w': _normal(ks[12], (L, CONV_KERNEL, CONV_WIDTH), CONV_KERNEL ** -0.5),
        'conv_dw_b': _normal(ks[13], (L, CONV_WIDTH), 0.02),
        'conv_ln_g': 1.0 + _normal(ks[14], (L, CONV_WIDTH), 0.02),
        'conv_ln_b': _normal(ks[15], (L, CONV_WIDTH), 0.02),
        'mem_ln_g': 1.0 + _normal(ks[16], (L, D), 0.02),
        'mem_ln_b': _normal(ks[17], (L, D), 0.02),
        'w_kv': _normal(ks[18], (L, D, 2 * ATTN_WIDTH), D ** -0.5),
        'w_out': _normal(ks[19], (L, MIX_WIDTH, D), DEEPNORM_BETA * MIX_WIDTH ** -0.5),
        'ln2_g': 1.0 + _normal(ks[20], (L, D), 0.02),
        'ln2_b': _normal(ks[21], (L, D), 0.02),
        'ffn2_w_gate': _normal(ks[22], (L, D, D_FF), D ** -0.5),
        'ffn2_w_up': _normal(ks[23], (L, D, D_FF), D ** -0.5),
        'ffn2_w_down': _normal(ks[24], (L, D_FF, D), DEEPNORM_BETA * D_FF ** -0.5),
        'ln3_g': 1.0 + _normal(ks[25], (L, D), 0.02),
        'ln3_b': _normal(ks[26], (L, D), 0.02),
    }


def reference(x_prompt, x_sample, mem_prompt, mem_sample,
              ffn1_w_gate, ffn1_w_up, ffn1_w_down, ln1_g, ln1_b,
              w_in, pool_w, pool_scale, conv_dw_w, conv_dw_b, conv_ln_g, conv_ln_b,
              mem_ln_g, mem_ln_b, w_kv, w_out, ln2_g, ln2_b,
              ffn2_w_gate, ffn2_w_up, ffn2_w_down, ln3_g, ln3_b):
    def trunk(x, mem):
        for l in range(DEPTH):
            x = encoder_layer(x, mem,
                              ffn1_w_gate[l], ffn1_w_up[l], ffn1_w_down[l], ln1_g[l], ln1_b[l],
                              w_in[l], pool_w[l], pool_scale[l], conv_dw_w[l], conv_dw_b[l], conv_ln_g[l], conv_ln_b[l],
                              mem_ln_g[l], mem_ln_b[l], w_kv[l], w_out[l], ln2_g[l], ln2_b[l],
                              ffn2_w_gate[l], ffn2_w_up[l], ffn2_w_down[l], ln3_g[l], ln3_b[l])
        return x

    y_prompt = trunk(x_prompt, mem_prompt)
    y_sample = trunk(x_sample, mem_sample)
    return (y_prompt, y_sample)
```

```python
import functools
import math

import jax
import jax.numpy as jnp
from jax import lax
from jax.experimental import pallas as pl
from jax.experimental.pallas import tpu as pltpu

LN_EPS = 1e-5
MACARON_WEIGHT = 0.5
POOL_WINDOWS = (2, 4, 8, 16)
N_XATTN_HEADS = 4

LANES = 128
HALO = 16
ROW_TILE = 512
FF_CHUNK = 256
CONV_ROWS = 64
VMEM_LIMIT_BYTES = 56 * 1024 * 1024

_BF16 = jnp.bfloat16
_F32 = jnp.float32


def _layer_norm(r, g, b):
    mu = jnp.mean(r, axis=-1, keepdims=True)
    rc = r - mu
    var = jnp.mean(rc * rc, axis=-1, keepdims=True)
    return rc * lax.rsqrt(var + LN_EPS) * g + b


def _swiglu_post_ln(x, wg_ref, wu_ref, wd_ref, g, b, h_scr, alpha):
    xb = x.astype(_BF16)
    d_ff = wg_ref.shape[1]
    for c0 in range(0, d_ff, FF_CHUNK):
        cols = slice(c0, c0 + FF_CHUNK)
        hg = jnp.dot(xb, wg_ref[:, cols], preferred_element_type=_F32)
        hu = jnp.dot(xb, wu_ref[:, cols], preferred_element_type=_F32)
        h_scr[:, cols] = (jax.nn.silu(hg) * hu).astype(_BF16)
    y = jnp.dot(h_scr[...], wd_ref[...], preferred_element_type=_F32)
    return _layer_norm(alpha * x + MACARON_WEIGHT * y, g, b)


def _kv_kernel(mem_ref, g_ref, b_ref, wkt_ref, wv_ref, kt_ref, vb_ref, *, scale):
    memn = _layer_norm(mem_ref[0], g_ref[...], b_ref[...]).astype(_BF16)
    kt = lax.dot_general(wkt_ref[...], memn, (((1,), (1,)), ((), ())),
                         preferred_element_type=_F32) * scale
    v = jnp.dot(memn, wv_ref[...], preferred_element_type=_F32)
    n_mem, width = v.shape
    head_dim = width // N_XATTN_HEADS
    feat_row = lax.broadcasted_iota(jnp.int32, (width, n_mem), 0)
    feat_col = lax.broadcasted_iota(jnp.int32, (n_mem, width), 1)
    for h in range(N_XATTN_HEADS):
        lo, hi = h * head_dim, (h + 1) * head_dim
        in_row = (feat_row >= lo) & (feat_row < hi)
        in_col = (feat_col >= lo) & (feat_col < hi)
        kt_ref[0, :, h * n_mem:(h + 1) * n_mem] = jnp.where(in_row, kt, 0.0).astype(_BF16)
        vb_ref[0, h * n_mem:(h + 1) * n_mem, :] = jnp.where(in_col, v, 0.0).astype(_BF16)


def _ffn_in_kernel(x_ref, wg_ref, wu_ref, wd_ref, g_ref, b_ref, win_ref,
                   x1_ref, z_ref, h_scr, *, alpha):
    x1 = _swiglu_post_ln(x_ref[0], wg_ref, wu_ref, wd_ref, g_ref[...], b_ref[...], h_scr, alpha)
    x1_ref[0] = x1
    z_ref[0] = jnp.dot(x1.astype(_BF16), win_ref[...], preferred_element_type=_F32)


def _pool_mixer(ubuf, poolw_ref, pscale_ref, t0, seq_len, rows):
    width = ubuf.shape[1]
    group = width // len(POOL_WINDOWS)

    def shifted(d):
        return ubuf[pl.ds(HALO + d, rows), :]

    u0 = shifted(0)
    sums = []
    acc = None
    lo_prev, hi_prev = 0, -1
    for w in POOL_WINDOWS:
        lo, hi = -(w // 2), w - 1 - w // 2
        for d in list(range(lo, lo_prev)) + list(range(hi_prev + 1, hi + 1)):
            term = u0 if d == 0 else shifted(d)
            acc = term if acc is None else acc + term
        lo_prev, hi_prev = lo, hi
        sums.append(acc)

    lane = lax.broadcasted_iota(jnp.int32, (rows, width), 1)
    t = t0 + lax.broadcasted_iota(jnp.int32, (rows, width), 0)
    win_sum = sums[-1]
    left = jnp.full((rows, width), POOL_WINDOWS[-1] // 2, jnp.int32)
    for gi in range(len(POOL_WINDOWS) - 2, -1, -1):
        in_group = lane < (gi + 1) * group
        win_sum = jnp.where(in_group, sums[gi], win_sum)
        left = jnp.where(in_group, POOL_WINDOWS[gi] // 2, left)
    count = jnp.minimum(t + left, seq_len) - jnp.maximum(t - left, 0)
    m = win_sum / count.astype(_F32) - u0
    y = jnp.dot(m.astype(_BF16), poolw_ref[...], preferred_element_type=_F32)
    return y * pscale_ref[...]


def _depthwise_conv(hbuf, dww_ref, dwb_ref, out_scr, rows):
    taps = dww_ref.shape[0]
    base = HALO - taps // 2
    for j in range(hbuf.shape[0]):
        lanes = slice(j * LANES, (j + 1) * LANES)
        w = dww_ref[:, lanes]
        bias = dwb_ref[:, lanes]
        for r0 in range(0, rows, CONV_ROWS):
            acc = jnp.broadcast_to(bias, (CONV_ROWS, LANES))
            for k in range(taps):
                acc = acc + hbuf[j, pl.ds(r0 + base + k, CONV_ROWS), :] * w[k:k + 1, :]
            out_scr[r0:r0 + CONV_ROWS, lanes] = acc


def _mix_ffn_kernel(x1_ref, z_ref, zp_ref, zn_ref, kt_ref, vb_ref,
                    poolw_ref, pscale_ref, dww_ref, dwb_ref, clng_ref, clnb_ref,
                    wout_ref, ln2g_ref, ln2b_ref, wg_ref, wu_ref, wd_ref, ln3g_ref, ln3b_ref,
                    o_ref, ubuf, hbuf, conv_scr, mix_scr, h_scr, *, alpha, seq_len):
    i = pl.program_id(1)
    rows = z_ref.shape[1]
    pw = poolw_ref.shape[0]
    cw = dww_ref.shape[1]
    n_mem = vb_ref.shape[1] // N_XATTN_HEADS
    first = i == 0
    last = i == pl.num_programs(1) - 1

    def glu(zz):
        return zz[:, pw:pw + cw] * jax.nn.sigmoid(zz[:, pw + cw:pw + 2 * cw])

    z = z_ref[0]
    zp = zp_ref[0]
    zn = zn_ref[0]
    u_prev = jnp.where(first, 0.0, zp[:, :pw])
    u_next = jnp.where(last, 0.0, zn[:, :pw])
    h_prev = jnp.where(first, 0.0, glu(zp))
    h_next = jnp.where(last, 0.0, glu(zn))
    h_cur = glu(z)
    ubuf[0:HALO, :] = u_prev
    ubuf[HALO:HALO + rows, :] = z[:, :pw]
    ubuf[HALO + rows:, :] = u_next
    for j in range(cw // LANES):
        lanes = slice(j * LANES, (j + 1) * LANES)
        hbuf[j, 0:HALO, :] = h_prev[:, lanes]
        hbuf[j, HALO:HALO + rows, :] = h_cur[:, lanes]
        hbuf[j, HALO + rows:, :] = h_next[:, lanes]

    y_pool = _pool_mixer(ubuf, poolw_ref, pscale_ref, i * rows, seq_len, rows)
    mix_scr[:, 0:pw] = y_pool.astype(_BF16)

    _depthwise_conv(hbuf, dww_ref, dwb_ref, conv_scr, rows)
    y_conv = jax.nn.silu(_layer_norm(conv_scr[...], clng_ref[...], clnb_ref[...]))
    mix_scr[:, pw:pw + cw] = y_conv.astype(_BF16)

    q = z[:, pw + 2 * cw:].astype(_BF16)
    s = jnp.dot(q, kt_ref[0], preferred_element_type=_F32)
    probs = []
    for h in range(N_XATTN_HEADS):
        sh = s[:, h * n_mem:(h + 1) * n_mem]
        e = jnp.exp(sh - jnp.max(sh, axis=-1, keepdims=True))
        probs.append((e / jnp.sum(e, axis=-1, keepdims=True)).astype(_BF16))
    p = jnp.concatenate(probs, axis=-1)
    y_attn = jnp.dot(p, vb_ref[0], preferred_element_type=_F32)
    mix_scr[:, pw + cw:] = y_attn.astype(_BF16)

    mix = jnp.dot(mix_scr[...], wout_ref[...], preferred_element_type=_F32)
    x2 = _layer_norm(alpha * x1_ref[0] + mix, ln2g_ref[...], ln2b_ref[...])
    o_ref[0] = _swiglu_post_ln(x2, wg_ref, wu_ref, wd_ref, ln3g_ref[...], ln3b_ref[...], h_scr, alpha)


def _const_spec(shape):
    zeros = (0,) * len(shape)
    return pl.BlockSpec(shape, lambda *_: zeros, pipeline_mode=pl.Buffered(1))


def _compiler_params(n_axes):
    return pltpu.CompilerParams(dimension_semantics=("arbitrary",) * n_axes,
                                vmem_limit_bytes=VMEM_LIMIT_BYTES)


def _row(v):
    return v.reshape(1, -1)


def _prepare_layer(p):
    bf = lambda w: w.astype(_BF16)
    aw = p["w_kv"].shape[1] // 2
    pool_w = p["pool_w"]
    q = {k: bf(p[k]) for k in ("ffn1_w_gate", "ffn1_w_up", "ffn1_w_down", "w_in", "w_out",
                               "ffn2_w_gate", "ffn2_w_up", "ffn2_w_down")}
    q.update({k: _row(p[k]) for k in ("ln1_g", "ln1_b", "pool_scale", "conv_dw_b", "conv_ln_g",
                                      "conv_ln_b", "mem_ln_g", "mem_ln_b", "ln2_g", "ln2_b",
                                      "ln3_g", "ln3_b")})
    q["conv_dw_w"] = p["conv_dw_w"]
    q["w_k_t"] = bf(p["w_kv"][:, :aw].T)
    q["w_v"] = bf(p["w_kv"][:, aw:])
    q["pool_w_bd"] = bf(jax.scipy.linalg.block_diag(*[pool_w[g] for g in range(pool_w.shape[0])]))
    return q


def _encoder_layer(x, mem, p, alpha):
    batch, seq_len, d_model = x.shape
    n_mem = mem.shape[1]
    d_ff = p["ffn1_w_gate"].shape[1]
    in_width = p["w_in"].shape[1]
    pw = p["pool_scale"].shape[1]
    cw = p["conv_dw_w"].shape[1]
    aw = p["w_v"].shape[1]
    rows = ROW_TILE
    assert seq_len % rows == 0 and rows % HALO == 0 and rows % CONV_ROWS == 0
    assert d_ff % FF_CHUNK == 0 and pw % LANES == 0 and cw % LANES == 0
    assert n_mem % LANES == 0 and aw % N_XATTN_HEADS == 0
    assert in_width == pw + 2 * cw + aw and pw + cw + aw == p["w_out"].shape[0]
    assert p["conv_dw_w"].shape[0] // 2 < HALO and POOL_WINDOWS[-1] // 2 <= HALO
    n_tiles = seq_len // rows
    halo_per_tile = rows // HALO
    n_halo_blocks = seq_len // HALO

    kt, vb = pl.pallas_call(
        functools.partial(_kv_kernel, scale=1.0 / math.sqrt(aw // N_XATTN_HEADS)),
        grid=(batch,),
        in_specs=[pl.BlockSpec((1, n_mem, d_model), lambda b: (b, 0, 0)),
                  _const_spec((1, d_model)), _const_spec((1, d_model)),
                  _const_spec((aw, d_model)), _const_spec((d_model, aw))],
        out_specs=[pl.BlockSpec((1, aw, N_XATTN_HEADS * n_mem), lambda b: (b, 0, 0)),
                   pl.BlockSpec((1, N_XATTN_HEADS * n_mem, aw), lambda b: (b, 0, 0))],
        out_shape=[jax.ShapeDtypeStruct((batch, aw, N_XATTN_HEADS * n_mem), _BF16),
                   jax.ShapeDtypeStruct((batch, N_XATTN_HEADS * n_mem, aw), _BF16)],
        compiler_params=_compiler_params(1),
        name="kv",
    )(mem, p["mem_ln_g"], p["mem_ln_b"], p["w_k_t"], p["w_v"])

    tile = lambda width: pl.BlockSpec((1, rows, width), lambda b, i: (b, i, 0))
    x1, z = pl.pallas_call(
        functools.partial(_ffn_in_kernel, alpha=alpha),
        grid=(batch, n_tiles),
        in_specs=[tile(d_model),
                  _const_spec((d_model, d_ff)), _const_spec((d_model, d_ff)),
                  _const_spec((d_ff, d_model)),
                  _const_spec((1, d_model)), _const_spec((1, d_model)),
                  _const_spec((d_model, in_width))],
        out_specs=[tile(d_model), tile(in_width)],
        out_shape=[jax.ShapeDtypeStruct((batch, seq_len, d_model), _F32),
                   jax.ShapeDtypeStruct((batch, seq_len, in_width), _F32)],
        scratch_shapes=[pltpu.VMEM((rows, d_ff), _BF16)],
        compiler_params=_compiler_params(2),
        name="ffn_in",
    )(x, p["ffn1_w_gate"], p["ffn1_w_up"], p["ffn1_w_down"], p["ln1_g"], p["ln1_b"], p["w_in"])

    halo_prev = pl.BlockSpec(
        (1, HALO, in_width), lambda b, i: (b, jnp.maximum(i * halo_per_tile - 1, 0), 0))
    halo_next = pl.BlockSpec(
        (1, HALO, in_width),
        lambda b, i: (b, jnp.minimum((i + 1) * halo_per_tile, n_halo_blocks - 1), 0))
    per_batch = lambda s0, s1: pl.BlockSpec((1, s0, s1), lambda b, i: (b, 0, 0))
    return pl.pallas_call(
        functools.partial(_mix_ffn_kernel, alpha=alpha, seq_len=seq_len),
        grid=(batch, n_tiles),
        in_specs=[tile(d_model), tile(in_width), halo_prev, halo_next,
                  per_batch(aw, N_XATTN_HEADS * n_mem), per_batch(N_XATTN_HEADS * n_mem, aw),
                  _const_spec((pw, pw)), _const_spec((1, pw)),
                  _const_spec(p["conv_dw_w"].shape), _const_spec((1, cw)),
                  _const_spec((1, cw)), _const_spec((1, cw)),
                  _const_spec((pw + cw + aw, d_model)),
                  _const_spec((1, d_model)), _const_spec((1, d_model)),
                  _const_spec((d_model, d_ff)), _const_spec((d_model, d_ff)),
                  _const_spec((d_ff, d_model)),
                  _const_spec((1, d_model)), _const_spec((1, d_model))],
        out_specs=tile(d_model),
        out_shape=jax.ShapeDtypeStruct((batch, seq_len, d_model), _F32),
        scratch_shapes=[pltpu.VMEM((rows + 2 * HALO, pw), _F32),
                        pltpu.VMEM((cw // LANES, rows + 2 * HALO, LANES), _F32),
                        pltpu.VMEM((rows, cw), _F32),
                        pltpu.VMEM((rows, pw + cw + aw), _BF16),
                        pltpu.VMEM((rows, d_ff), _BF16)],
        compiler_params=_compiler_params(2),
        name="mix",
    )(x1, z, z, z, kt, vb,
      p["pool_w_bd"], p["pool_scale"], p["conv_dw_w"], p["conv_dw_b"],
      p["conv_ln_g"], p["conv_ln_b"], p["w_out"], p["ln2_g"], p["ln2_b"],
      p["ffn2_w_gate"], p["ffn2_w_up"], p["ffn2_w_down"], p["ln3_g"], p["ln3_b"])


_PARAM_NAMES = (
    "ffn1_w_gate", "ffn1_w_up", "ffn1_w_down", "ln1_g", "ln1_b",
    "w_in", "pool_w", "pool_scale", "conv_dw_w", "conv_dw_b", "conv_ln_g", "conv_ln_b",
    "mem_ln_g", "mem_ln_b", "w_kv", "w_out", "ln2_g", "ln2_b",
    "ffn2_w_gate", "ffn2_w_up", "ffn2_w_down", "ln3_g", "ln3_b")


def kernel(x_prompt, x_sample, mem_prompt, mem_sample, ffn1_w_gate, ffn1_w_up, ffn1_w_down, ln1_g, ln1_b, w_in, pool_w, pool_scale, conv_dw_w, conv_dw_b, conv_ln_g, conv_ln_b, mem_ln_g, mem_ln_b, w_kv, w_out, ln2_g, ln2_b, ffn2_w_gate, ffn2_w_up, ffn2_w_down, ln3_g, ln3_b):
    stacked = dict(zip(_PARAM_NAMES, (
        ffn1_w_gate, ffn1_w_up, ffn1_w_down, ln1_g, ln1_b,
        w_in, pool_w, pool_scale, conv_dw_w, conv_dw_b, conv_ln_g, conv_ln_b,
        mem_ln_g, mem_ln_b, w_kv, w_out, ln2_g, ln2_b,
        ffn2_w_gate, ffn2_w_up, ffn2_w_down, ln3_g, ln3_b)))
    depth = ffn1_w_gate.shape[0]
    alpha = (2.0 * depth) ** 0.25
    layers = [_prepare_layer({k: v[layer] for k, v in stacked.items()}) for layer in range(depth)]

    def trunk(x, mem):
        for p in layers:
            x = _encoder_layer(x, mem, p, alpha)
        return x

    return (trunk(x_prompt, mem_prompt), trunk(x_sample, mem_sample))
```

```python
import functools
import math

import jax
import jax.numpy as jnp
from jax import lax
from jax.experimental import pallas as pl
from jax.experimental.pallas import tpu as pltpu

LN_EPS = 1e-5
MACARON_WEIGHT = 0.5
POOL_WINDOWS = (2, 4, 8, 16)
N_XATTN_HEADS = 4

LANES = 128
SUBLANES = 8
MXU_WIDTH = 256
HALO = 16
ROW_TILE = 512
FF_CHUNK = 256
MIX_ROWS = 64
NORM_ROWS = 128
PIN_LAG = 2
VMEM_LIMIT_BYTES = 56 * 1024 * 1024

_BF16 = jnp.bfloat16
_F32 = jnp.float32


def _layer_norm(r, g, b):
    mu = jnp.mean(r, axis=-1, keepdims=True)
    rc = r - mu
    var = jnp.mean(rc * rc, axis=-1, keepdims=True)
    return rc * lax.rsqrt(var + LN_EPS) * g + b


def _zero_tile(values):
    word = None
    for v in values:
        bits = pltpu.bitcast(v, jnp.uint32)
        for r in range(0, bits.shape[0], SUBLANES):
            for c in range(0, bits.shape[1], LANES):
                blk = bits[r:r + SUBLANES, c:c + LANES]
                word = blk if word is None else word | blk
    sixteen = jnp.uint32(16)
    word = lax.shift_right_logical(lax.shift_right_logical(word, sixteen), sixteen)
    return pltpu.bitcast(word, _F32)


def _add_into_tile(ref, index, zero):
    tile = ref[index].astype(_F32)
    zero = jnp.concatenate([zero] * (tile.shape[0] // SUBLANES), axis=0)
    ref[index] = (tile + zero).astype(ref.dtype)


def _first_tile(ref):
    rows = SUBLANES * 4 // ref.dtype.itemsize
    return (slice(0, rows), slice(0, LANES))


class _Filler:
    def __init__(self, pieces, total_weight):
        self._pieces = list(pieces)
        self._total_cost = sum(c for c, _ in self._pieces)
        self._total_weight = total_weight
        self._weight = 0.0
        self._cost = 0.0

    def emit(self, weight, gate):
        self._weight += weight
        target = self._total_cost * min(self._weight / self._total_weight, 1.0)
        produced = []
        while self._pieces and self._cost + 0.5 * self._pieces[0][0] <= target:
            cost, thunk = self._pieces.pop(0)
            value = thunk(gate)
            if value is not None:
                produced.append(value)
            self._cost += cost
        return produced

    def done(self):
        return not self._pieces


def _swiglu_residual(x, wg_ref, wu_ref, wd_ref, h_scr, alpha, after_first_chunk=None):
    xb = x.astype(_BF16)
    d_ff = wg_ref.shape[1]
    for c0 in range(0, d_ff, FF_CHUNK):
        cols = slice(c0, c0 + FF_CHUNK)
        hg = jnp.dot(xb, wg_ref[:, cols], preferred_element_type=_F32)
        hu = jnp.dot(xb, wu_ref[:, cols], preferred_element_type=_F32)
        h_scr[:, cols] = (jax.nn.silu(hg) * hu).astype(_BF16)
        if c0 == 0 and after_first_chunk is not None:
            after_first_chunk(hg)
    y = jnp.dot(h_scr[...], wd_ref[...], preferred_element_type=_F32)
    return alpha * x + MACARON_WEIGHT * y


_UP_CHUNK_WEIGHT = 1.0
_DOWN_CHUNK_WEIGHT = 1.375
_OUT_PROJ_WEIGHT = 2.5


def _ffn_steps(d_ff, d_model):
    return ([("up", c0) for c0 in range(0, d_ff, FF_CHUNK)]
            + [("down", n0) for n0 in range(0, d_model, MXU_WIDTH)])


def _filler_weight(d_ff, d_model):
    steps = _ffn_steps(d_ff, d_model)[:-PIN_LAG]
    return _OUT_PROJ_WEIGHT + sum(
        _UP_CHUNK_WEIGHT if kind == "up" else _DOWN_CHUNK_WEIGHT for kind, _ in steps)


def _swiglu_residual_filled(x, wg_ref, wu_ref, wd_ref, xb_scr, h_scr, alpha, filler):
    d_ff, d_model = wd_ref.shape
    steps = _ffn_steps(d_ff, d_model)
    pending = {}
    gate_source = x
    parts = []
    for s, (kind, off) in enumerate(steps):
        lhs_ref = xb_scr if kind == "up" else h_scr
        if pending.get(s):
            _add_into_tile(lhs_ref, _first_tile(lhs_ref), _zero_tile(pending.pop(s)))
        produced = []
        if kind == "up":
            cols = slice(off, off + FF_CHUNK)
            hg = jnp.dot(xb_scr[...], wg_ref[:, cols], preferred_element_type=_F32)
            hu = jnp.dot(xb_scr[...], wu_ref[:, cols], preferred_element_type=_F32)
            h = (jax.nn.silu(hg) * hu).astype(_BF16)
            h_scr[:, cols] = h
            produced.append(h)
            result = hg
        else:
            result = jnp.dot(h_scr[...], wd_ref[:, off:off + MXU_WIDTH],
                             preferred_element_type=_F32)
            parts.append(result)
        if s + PIN_LAG < len(steps):
            gate = _zero_tile([gate_source[0:SUBLANES, 0:LANES]])
            produced += filler.emit(
                _UP_CHUNK_WEIGHT if kind == "up" else _DOWN_CHUNK_WEIGHT, gate)
            pending.setdefault(s + PIN_LAG, []).extend(produced)
        gate_source = result
    y = jnp.concatenate(parts, axis=-1)
    return alpha * x + MACARON_WEIGHT * y


def _kv_kernel(mem_ref, g_ref, b_ref, wkt_ref, wv_ref, kt_ref, vb_ref, *, scale):
    memn = _layer_norm(mem_ref[0], g_ref[...], b_ref[...]).astype(_BF16)
    kt = lax.dot_general(wkt_ref[...], memn, (((1,), (1,)), ((), ())),
                         preferred_element_type=_F32) * scale
    v = jnp.dot(memn, wv_ref[...], preferred_element_type=_F32)
    n_mem, width = v.shape
    head_dim = width // N_XATTN_HEADS
    feat_row = lax.broadcasted_iota(jnp.int32, (width, n_mem), 0)
    feat_col = lax.broadcasted_iota(jnp.int32, (n_mem, width), 1)
    for h in range(N_XATTN_HEADS):
        lo, hi = h * head_dim, (h + 1) * head_dim
        in_row = (feat_row >= lo) & (feat_row < hi)
        in_col = (feat_col >= lo) & (feat_col < hi)
        kt_ref[0, :, h * n_mem:(h + 1) * n_mem] = jnp.where(in_row, kt, 0.0).astype(_BF16)
        vb_ref[0, h * n_mem:(h + 1) * n_mem, :] = jnp.where(in_col, v, 0.0).astype(_BF16)


def _ffn_in_kernel(x_ref, wg_ref, wu_ref, wd_ref, g_ref, b_ref, win_ref,
                   x1_ref, z_ref, r_scr, h_scr, *, alpha):
    @pl.when(pl.program_id(0) == 0)
    def _():
        r_scr[...] = jnp.zeros_like(r_scr)

    def previous_tile(first_result):
        zero_row = jnp.concatenate(
            [_zero_tile([first_result[0:SUBLANES, 0:LANES]])[0:1, :]] * (g_ref.shape[1] // LANES),
            axis=1)
        x1 = _layer_norm(r_scr[...], g_ref[...] + zero_row, b_ref[...])
        x1_ref[0] = x1
        z_ref[0] = jnp.dot(x1.astype(_BF16), win_ref[...], preferred_element_type=_F32)

    r = _swiglu_residual(x_ref[0], wg_ref, wu_ref, wd_ref, h_scr, alpha, previous_tile)
    r_scr[...] = r


def _pool_pieces(ubuf, m_scr, t0, seq_len, rows):
    n_groups = len(POOL_WINDOWS)
    group = ubuf.shape[0] * LANES // n_groups
    pieces = []
    for j in range(ubuf.shape[0]):
        groups = [g for g in range(n_groups)
                  if g * group < (j + 1) * LANES and (g + 1) * group > j * LANES]

        def piece(gate, j=j, groups=groups, r0=0):
            if gate is not None:
                _add_into_tile(ubuf, (j, slice(HALO + r0, HALO + r0 + SUBLANES), slice(None)), gate)
            lane = j * LANES + lax.broadcasted_iota(jnp.int32, (MIX_ROWS, LANES), 1)
            row = lax.broadcasted_iota(jnp.int32, (MIX_ROWS, LANES), 0)
            in_group = [lane < (g + 1) * group for g in groups[:-1]]
            u0 = ubuf[j, pl.ds(HALO + r0, MIX_ROWS), :]
            sums = {}
            acc = None
            lo_prev, hi_prev = 0, -1
            for g in range(groups[-1] + 1):
                w = POOL_WINDOWS[g]
                lo, hi = -(w // 2), w - 1 - w // 2
                for d in list(range(lo, lo_prev)) + list(range(hi_prev + 1, hi + 1)):
                    term = u0 if d == 0 else ubuf[j, pl.ds(HALO + r0 + d, MIX_ROWS), :]
                    acc = term if acc is None else acc + term
                lo_prev, hi_prev = lo, hi
                sums[g] = acc
            win_sum = sums[groups[-1]]
            left = jnp.full((MIX_ROWS, LANES), POOL_WINDOWS[groups[-1]] // 2, jnp.int32)
            for g, mask in reversed(list(zip(groups[:-1], in_group))):
                win_sum = jnp.where(mask, sums[g], win_sum)
                left = jnp.where(mask, POOL_WINDOWS[g] // 2, left)
            t = t0 + r0 + row
            count = jnp.minimum(t + left, seq_len) - jnp.maximum(t - left, 0)
            m = (win_sum / count.astype(_F32) - u0).astype(_BF16)
            m_scr[r0:r0 + MIX_ROWS, j * LANES:(j + 1) * LANES] = m
            return m

        cost = (MIX_ROWS // SUBLANES) * (POOL_WINDOWS[groups[-1]] + 24)
        for r0 in range(0, rows, MIX_ROWS):
            pieces.append((cost, functools.partial(piece, r0=r0)))
    return pieces


def _conv_pieces(hbuf, dww_ref, dwb_ref, out_scr, rows):
    taps = dww_ref.shape[0]
    base = HALO - taps // 2

    def piece(j, r0, gate):
        if gate is not None:
            _add_into_tile(hbuf, (j, slice(HALO + r0, HALO + r0 + SUBLANES), slice(None)), gate)
        lanes = slice(j * LANES, (j + 1) * LANES)
        w = dww_ref[:, lanes]
        acc = jnp.broadcast_to(dwb_ref[:, lanes], (MIX_ROWS, LANES))
        for k in range(taps):
            acc = acc + hbuf[j, pl.ds(r0 + base + k, MIX_ROWS), :] * w[k:k + 1, :]
        out_scr[r0:r0 + MIX_ROWS, lanes] = acc

    cost = (MIX_ROWS // SUBLANES) * 2 * taps
    return {r0: [(cost, functools.partial(piece, j, r0)) for j in range(hbuf.shape[0])]
            for r0 in range(0, rows, MIX_ROWS)}


def _mixer_pieces(z_ref, zp_ref, zn_ref, kt_ref, vb_ref, poolw_ref, pscale_ref, dww_ref, dwb_ref,
                  clng_ref, clnb_ref, ubuf, hbuf, conv_scr, m_scr, s_scr, p_scr, mix_scr,
                  i, n_tiles, seq_len):
    rows = z_ref.shape[1]
    pw = poolw_ref.shape[0]
    cw = dww_ref.shape[1]
    n_mem = vb_ref.shape[1] // N_XATTN_HEADS
    first = i == 0
    last = i == n_tiles - 1
    row_vregs = rows // SUBLANES
    pieces = []

    def glu_block(j, gate):
        a_cols = slice(pw + j * LANES, pw + (j + 1) * LANES)
        g_cols = slice(pw + cw + j * LANES, pw + cw + (j + 1) * LANES)
        glu = lambda ref: ref[0, :, a_cols] * jax.nn.sigmoid(ref[0, :, g_cols])
        hbuf[j, 0:HALO, :] = jnp.where(first, 0.0, glu(zp_ref))
        hbuf[j, HALO:HALO + rows, :] = glu(z_ref)
        hbuf[j, HALO + rows:, :] = jnp.where(last, 0.0, glu(zn_ref))

    def pool_block(j, gate):
        cols = slice(j * LANES, (j + 1) * LANES)
        ubuf[j, 0:HALO, :] = jnp.where(first, 0.0, zp_ref[0, :, cols])
        ubuf[j, HALO:HALO + rows, :] = z_ref[0, :, cols]
        ubuf[j, HALO + rows:, :] = jnp.where(last, 0.0, zn_ref[0, :, cols])

    def scores(gate):
        q = z_ref[0, :, pw + 2 * cw:].astype(_BF16)
        s_scr[...] = jnp.dot(q, kt_ref[0], preferred_element_type=_F32)

    def softmax(h, r0, gate):
        if gate is not None:
            _add_into_tile(s_scr, (slice(r0, r0 + SUBLANES),
                                  slice(h * n_mem, h * n_mem + LANES)), gate)
        sh = s_scr[r0:r0 + NORM_ROWS, h * n_mem:(h + 1) * n_mem]
        e = jnp.exp(sh - jnp.max(sh, axis=-1, keepdims=True))
        p = (e / jnp.sum(e, axis=-1, keepdims=True)).astype(_BF16)
        p_scr[r0:r0 + NORM_ROWS, h * n_mem:(h + 1) * n_mem] = p
        return p

    def attn_out(gate):
        y = jnp.dot(p_scr[...], vb_ref[0], preferred_element_type=_F32).astype(_BF16)
        mix_scr[:, pw + cw:] = y
        return y

    def pool_out(gate):
        y = jnp.dot(m_scr[...], poolw_ref[...], preferred_element_type=_F32)
        y = (y * pscale_ref[...]).astype(_BF16)
        mix_scr[:, 0:pw] = y
        return y

    def conv_out(r0, gate):
        c = conv_scr[r0:r0 + MIX_ROWS, :]
        y = jax.nn.silu(_layer_norm(c, clng_ref[...], clnb_ref[...])).astype(_BF16)
        mix_scr[r0:r0 + MIX_ROWS, pw:pw + cw] = y
        return y

    pieces += [(row_vregs, functools.partial(pool_block, j)) for j in range(pw // LANES)]
    pieces.append((row_vregs * 2, scores))
    pieces += [(row_vregs * 8, functools.partial(glu_block, j)) for j in range(cw // LANES)]
    pieces += _pool_pieces(ubuf, m_scr, i * rows, seq_len, rows)
    pieces.append((row_vregs * 4, pool_out))
    pieces += [((NORM_ROWS // SUBLANES) * (n_mem // LANES) * 8, functools.partial(softmax, h, r0))
               for h in range(N_XATTN_HEADS) for r0 in range(0, rows, NORM_ROWS)]
    pieces.append((row_vregs * 2, attn_out))
    for r0, conv_pieces in _conv_pieces(hbuf, dww_ref, dwb_ref, conv_scr, rows).items():
        pieces += conv_pieces
        pieces.append(((MIX_ROWS // SUBLANES) * (cw // LANES) * 16,
                       functools.partial(conv_out, r0)))
    return pieces


def _mix_ffn_kernel(x1_ref, z_ref, zp_ref, zn_ref, kt_ref, vb_ref,
                    poolw_ref, pscale_ref, dww_ref, dwb_ref, clng_ref, clnb_ref,
                    wout_ref, ln2g_ref, ln2b_ref, wg_ref, wu_ref, wd_ref, ln3g_ref, ln3b_ref,
                    o_ref, ubuf, hbuf, conv_scr, m_scr, s_scr, p_scr, mix_scr, xb_scr, h_scr, r_scr,
                    *, alpha, seq_len, n_tiles, n_total):
    t = pl.program_id(0)

    @pl.when(t == 0)
    def _():
        mix_scr[...] = jnp.zeros_like(mix_scr)
        r_scr[...] = jnp.zeros_like(r_scr)

    def final_ln(r0, gate):
        y = _layer_norm(r_scr[r0:r0 + MIX_ROWS, :], ln3g_ref[...], ln3b_ref[...])
        o_ref[0, r0:r0 + MIX_ROWS, :] = y
        return y

    rows = z_ref.shape[1]
    i = lax.rem(jnp.minimum(t, n_total - 1), n_tiles)
    d_ff, d_model = wd_ref.shape
    ln_cost = (MIX_ROWS // SUBLANES) * (d_model // LANES) * 10
    filler = _Filler(
        [(ln_cost, functools.partial(final_ln, r0)) for r0 in range(0, rows, MIX_ROWS)]
        + _mixer_pieces(z_ref, zp_ref, zn_ref, kt_ref, vb_ref, poolw_ref, pscale_ref, dww_ref,
                        dwb_ref, clng_ref, clnb_ref, ubuf, hbuf, conv_scr, m_scr, s_scr, p_scr,
                        mix_scr, i, n_tiles, seq_len),
        total_weight=_filler_weight(d_ff, d_model))

    mix = jnp.dot(mix_scr[...], wout_ref[...], preferred_element_type=_F32)
    head = filler.emit(_OUT_PROJ_WEIGHT, None)
    alpha_anchored = alpha + _zero_tile(head)[0:1, 0:1]
    x2 = _layer_norm(alpha_anchored * x1_ref[0] + mix, ln2g_ref[...], ln2b_ref[...])
    xb_scr[...] = x2.astype(_BF16)
    r = _swiglu_residual_filled(x2, wg_ref, wu_ref, wd_ref, xb_scr, h_scr, alpha, filler)
    assert filler.done()
    r_scr[...] = r


def _const_spec(shape):
    zeros = (0,) * len(shape)
    return pl.BlockSpec(shape, lambda *_: zeros, pipeline_mode=pl.Buffered(1))


def _compiler_params(n_axes):
    return pltpu.CompilerParams(dimension_semantics=("arbitrary",) * n_axes,
                                vmem_limit_bytes=VMEM_LIMIT_BYTES)


def _row(v):
    return v.reshape(1, -1)


def _prepare_layer(p):
    bf = lambda w: w.astype(_BF16)
    aw = p["w_kv"].shape[1] // 2
    pool_w = p["pool_w"]
    q = {k: bf(p[k]) for k in ("ffn1_w_gate", "ffn1_w_up", "ffn1_w_down", "w_in", "w_out",
                               "ffn2_w_gate", "ffn2_w_up", "ffn2_w_down")}
    q.update({k: _row(p[k]) for k in ("ln1_g", "ln1_b", "pool_scale", "conv_dw_b", "conv_ln_g",
                                      "conv_ln_b", "mem_ln_g", "mem_ln_b", "ln2_g", "ln2_b",
                                      "ln3_g", "ln3_b")})
    q["conv_dw_w"] = p["conv_dw_w"]
    q["w_k_t"] = bf(p["w_kv"][:, :aw].T)
    q["w_v"] = bf(p["w_kv"][:, aw:])
    q["pool_w_bd"] = bf(jax.scipy.linalg.block_diag(*[pool_w[g] for g in range(pool_w.shape[0])]))
    return q


def _encoder_layer(x, mem, p, alpha):
    batch, seq_len, d_model = x.shape
    n_mem = mem.shape[1]
    d_ff = p["ffn1_w_gate"].shape[1]
    in_width = p["w_in"].shape[1]
    pw = p["pool_scale"].shape[1]
    cw = p["conv_dw_w"].shape[1]
    aw = p["w_v"].shape[1]
    rows = ROW_TILE
    assert seq_len % rows == 0 and rows % HALO == 0
    assert rows % MIX_ROWS == 0 and rows % NORM_ROWS == 0
    assert d_ff % FF_CHUNK == 0 and d_model % MXU_WIDTH == 0
    assert pw % LANES == 0 and cw % LANES == 0
    assert n_mem % LANES == 0 and aw % N_XATTN_HEADS == 0
    assert in_width == pw + 2 * cw + aw and pw + cw + aw == p["w_out"].shape[0]
    assert p["conv_dw_w"].shape[0] // 2 < HALO and POOL_WINDOWS[-1] // 2 <= HALO
    n_tiles = seq_len // rows
    n_total = batch * n_tiles
    halo_per_tile = rows // HALO
    n_halo_blocks = seq_len // HALO

    kt, vb = pl.pallas_call(
        functools.partial(_kv_kernel, scale=1.0 / math.sqrt(aw // N_XATTN_HEADS)),
        grid=(batch,),
        in_specs=[pl.BlockSpec((1, n_mem, d_model), lambda b: (b, 0, 0)),
                  _const_spec((1, d_model)), _const_spec((1, d_model)),
                  _const_spec((aw, d_model)), _const_spec((d_model, aw))],
        out_specs=[pl.BlockSpec((1, aw, N_XATTN_HEADS * n_mem), lambda b: (b, 0, 0)),
                   pl.BlockSpec((1, N_XATTN_HEADS * n_mem, aw), lambda b: (b, 0, 0))],
        out_shape=[jax.ShapeDtypeStruct((batch, aw, N_XATTN_HEADS * n_mem), _BF16),
                   jax.ShapeDtypeStruct((batch, N_XATTN_HEADS * n_mem, aw), _BF16)],
        compiler_params=_compiler_params(1),
        name="kv",
    )(mem, p["mem_ln_g"], p["mem_ln_b"], p["w_k_t"], p["w_v"])

    def staged_tile(t, lag):
        tile_id = jnp.clip(t - lag, 0, n_total - 1)
        return tile_id // n_tiles, tile_id % n_tiles

    def staged(width, lag):
        return pl.BlockSpec((1, rows, width), lambda t: (*staged_tile(t, lag), 0))

    x1, z = pl.pallas_call(
        functools.partial(_ffn_in_kernel, alpha=alpha),
        grid=(n_total + 1,),
        in_specs=[staged(d_model, 0),
                  _const_spec((d_model, d_ff)), _const_spec((d_model, d_ff)),
                  _const_spec((d_ff, d_model)),
                  _const_spec((1, d_model)), _const_spec((1, d_model)),
                  _const_spec((d_model, in_width))],
        out_specs=[staged(d_model, 1), staged(in_width, 1)],
        out_shape=[jax.ShapeDtypeStruct((batch, seq_len, d_model), _F32),
                   jax.ShapeDtypeStruct((batch, seq_len, in_width), _F32)],
        scratch_shapes=[pltpu.VMEM((rows, d_model), _F32),
                        pltpu.VMEM((rows, d_ff), _BF16)],
        compiler_params=_compiler_params(1),
        name="ffn_in",
    )(x, p["ffn1_w_gate"], p["ffn1_w_up"], p["ffn1_w_down"], p["ln1_g"], p["ln1_b"], p["w_in"])

    def halo_prev_map(t):
        b, i = staged_tile(t, 0)
        return b, jnp.maximum(i * halo_per_tile - 1, 0), 0

    def halo_next_map(t):
        b, i = staged_tile(t, 0)
        return b, jnp.minimum((i + 1) * halo_per_tile, n_halo_blocks - 1), 0

    per_batch = lambda s0, s1: pl.BlockSpec((1, s0, s1), lambda t: (staged_tile(t, 0)[0], 0, 0))
    return pl.pallas_call(
        functools.partial(_mix_ffn_kernel, alpha=alpha, seq_len=seq_len,
                          n_tiles=n_tiles, n_total=n_total),
        grid=(n_total + 2,),
        in_specs=[staged(d_model, 1),
                  staged(in_width, 0),
                  pl.BlockSpec((1, HALO, in_width), halo_prev_map),
                  pl.BlockSpec((1, HALO, in_width), halo_next_map),
                  per_batch(aw, N_XATTN_HEADS * n_mem), per_batch(N_XATTN_HEADS * n_mem, aw),
                  _const_spec((pw, pw)), _const_spec((1, pw)),
                  _const_spec(p["conv_dw_w"].shape), _const_spec((1, cw)),
                  _const_spec((1, cw)), _const_spec((1, cw)),
                  _const_spec((pw + cw + aw, d_model)),
                  _const_spec((1, d_model)), _const_spec((1, d_model)),
                  _const_spec((d_model, d_ff)), _const_spec((d_model, d_ff)),
                  _const_spec((d_ff, d_model)),
                  _const_spec((1, d_model)), _const_spec((1, d_model))],
        out_specs=staged(d_model, 2),
        out_shape=jax.ShapeDtypeStruct((batch, seq_len, d_model), _F32),
        scratch_shapes=[pltpu.VMEM((pw // LANES, rows + 2 * HALO, LANES), _F32),
                        pltpu.VMEM((cw // LANES, rows + 2 * HALO, LANES), _F32),
                        pltpu.VMEM((rows, cw), _F32),
                        pltpu.VMEM((rows, pw), _BF16),
                        pltpu.VMEM((rows, N_XATTN_HEADS * n_mem), _F32),
                        pltpu.VMEM((rows, N_XATTN_HEADS * n_mem), _BF16),
                        pltpu.VMEM((rows, pw + cw + aw), _BF16),
                        pltpu.VMEM((rows, d_model), _BF16),
                        pltpu.VMEM((rows, d_ff), _BF16),
                        pltpu.VMEM((rows, d_model), _F32)],
        compiler_params=_compiler_params(1),
        name="mix",
    )(x1, z, z, z, kt, vb,
      p["pool_w_bd"], p["pool_scale"], p["conv_dw_w"], p["conv_dw_b"],
      p["conv_ln_g"], p["conv_ln_b"], p["w_out"], p["ln2_g"], p["ln2_b"],
      p["ffn2_w_gate"], p["ffn2_w_up"], p["ffn2_w_down"], p["ln3_g"], p["ln3_b"])


_PARAM_NAMES = (
    "ffn1_w_gate", "ffn1_w_up", "ffn1_w_down", "ln1_g", "ln1_b",
    "w_in", "pool_w", "pool_scale", "conv_dw_w", "conv_dw_b", "conv_ln_g", "conv_ln_b",
    "mem_ln_g", "mem_ln_b", "w_kv", "w_out", "ln2_g", "ln2_b",
    "ffn2_w_gate", "ffn2_w_up", "ffn2_w_down", "ln3_g", "ln3_b")


def kernel(x_prompt, x_sample, mem_prompt, mem_sample, ffn1_w_gate, ffn1_w_up, ffn1_w_down, ln1_g, ln1_b, w_in, pool_w, pool_scale, conv_dw_w, conv_dw_b, conv_ln_g, conv_ln_b, mem_ln_g, mem_ln_b, w_kv, w_out, ln2_g, ln2_b, ffn2_w_gate, ffn2_w_up, ffn2_w_down, ln3_g, ln3_b):
    stacked = dict(zip(_PARAM_NAMES, (
        ffn1_w_gate, ffn1_w_up, ffn1_w_down, ln1_g, ln1_b,
        w_in, pool_w, pool_scale, conv_dw_w, conv_dw_b, conv_ln_g, conv_ln_b,
        mem_ln_g, mem_ln_b, w_kv, w_out, ln2_g, ln2_b,
        ffn2_w_gate, ffn2_w_up, ffn2_w_down, ln3_g, ln3_b)))
    depth = ffn1_w_gate.shape[0]
    alpha = (2.0 * depth) ** 0.25
    layers = [_prepare_layer({k: v[layer] for k, v in stacked.items()}) for layer in range(depth)]

    def trunk(x, mem):
        for p in layers:
            x = _encoder_layer(x, mem, p, alpha)
        return x

    return (trunk(x_prompt, mem_prompt), trunk(x_sample, mem_sample))
```

```python
import functools
import math

import jax
import jax.numpy as jnp
from jax import lax
from jax.experimental import pallas as pl
from jax.experimental.pallas import tpu as pltpu

LN_EPS = 1e-5
MACARON_WEIGHT = 0.5
POOL_WINDOWS = (2, 4, 8, 16)
N_XATTN_HEADS = 4

LANES = 128
SUBLANES = 8
MXU_WIDTH = 256
HALO = 16
ROW_TILE = 512
FF_CHUNK = 256
MIX_ROWS = 64
NORM_ROWS = 128
PIN_LAG = 4
VMEM_LIMIT_BYTES = 56 * 1024 * 1024

_BF16 = jnp.bfloat16
_F32 = jnp.float32


def _layer_norm(r, g, b):
    mu = jnp.mean(r, axis=-1, keepdims=True)
    rc = r - mu
    var = jnp.mean(rc * rc, axis=-1, keepdims=True)
    return rc * lax.rsqrt(var + LN_EPS) * g + b


def _zero_tile(values):
    word = None
    for v in values:
        bits = pltpu.bitcast(v, jnp.uint32)
        for r in range(0, bits.shape[0], SUBLANES):
            for c in range(0, bits.shape[1], LANES):
                blk = bits[r:r + SUBLANES, c:c + LANES]
                word = blk if word is None else word | blk
    sixteen = jnp.uint32(16)
    word = lax.shift_right_logical(lax.shift_right_logical(word, sixteen), sixteen)
    return pltpu.bitcast(word, _F32)


def _add_into_tile(ref, index, zero):
    tile = ref[index].astype(_F32)
    zero = jnp.concatenate([zero] * (tile.shape[0] // SUBLANES), axis=0)
    ref[index] = (tile + zero).astype(ref.dtype)


def _first_tile(ref):
    rows = SUBLANES * 4 // ref.dtype.itemsize
    return (slice(0, rows), slice(0, LANES))


def _zero_fill(ref):
    rows = 8 * SUBLANES

    def body(k, carry):
        start = pl.multiple_of(k * rows, rows)
        ref[pl.ds(start, rows), :] = jnp.zeros((rows, ref.shape[1]), ref.dtype)
        return carry

    lax.fori_loop(0, ref.shape[0] // rows, body, 0)


class _Filler:
    def __init__(self, pieces, total_weight):
        self._pieces = list(pieces)
        self._total_cost = sum(c for c, _ in self._pieces)
        self._total_weight = total_weight
        self._weight = 0.0
        self._cost = 0.0

    def emit(self, weight, gate):
        self._weight += weight
        target = self._total_cost * min(self._weight / self._total_weight, 1.0)
        produced = []
        while self._pieces and self._cost + 0.5 * self._pieces[0][0] <= target:
            cost, thunk = self._pieces.pop(0)
            value = thunk(gate)
            if value is not None:
                produced.append(value)
            self._cost += cost
        return produced

    def done(self):
        return not self._pieces


def _swiglu_residual(x, wg_ref, wu_ref, wd_ref, h_scr, alpha, after_first_chunk=None):
    xb = x.astype(_BF16)
    d_ff = wg_ref.shape[1]
    for c0 in range(0, d_ff, FF_CHUNK):
        cols = slice(c0, c0 + FF_CHUNK)
        hg = jnp.dot(xb, wg_ref[:, cols], preferred_element_type=_F32)
        hu = jnp.dot(xb, wu_ref[:, cols], preferred_element_type=_F32)
        h_scr[:, cols] = (jax.nn.silu(hg) * hu).astype(_BF16)
        if c0 == 0 and after_first_chunk is not None:
            after_first_chunk(hg)
    y = jnp.dot(h_scr[...], wd_ref[...], preferred_element_type=_F32)
    return alpha * x + MACARON_WEIGHT * y


_UP_CHUNK_WEIGHT = 1.0
_DOWN_CHUNK_WEIGHT = 1.375
_OUT_PROJ_WEIGHT = 2.5


def _ffn_steps(d_ff, d_model):
    return ([("up", c0) for c0 in range(0, d_ff, FF_CHUNK)]
            + [("down", n0) for n0 in range(0, d_model, MXU_WIDTH)])


def _filler_weight(d_ff, d_model):
    steps = _ffn_steps(d_ff, d_model)[:-PIN_LAG]
    return _OUT_PROJ_WEIGHT + sum(
        _UP_CHUNK_WEIGHT if kind == "up" else _DOWN_CHUNK_WEIGHT for kind, _ in steps)


def _swiglu_residual_filled(x, wg_ref, wu_ref, wd_ref, xb_scr, h_scr, alpha, filler):
    d_ff, d_model = wd_ref.shape
    steps = _ffn_steps(d_ff, d_model)
    pending = {}
    gate_source = x
    parts = []
    for s, (kind, off) in enumerate(steps):
        lhs_ref = xb_scr if kind == "up" else h_scr
        if pending.get(s):
            _add_into_tile(lhs_ref, _first_tile(lhs_ref), _zero_tile(pending.pop(s)))
        produced = []
        if kind == "up":
            cols = slice(off, off + FF_CHUNK)
            hg = jnp.dot(xb_scr[...], wg_ref[:, cols], preferred_element_type=_F32)
            hu = jnp.dot(xb_scr[...], wu_ref[:, cols], preferred_element_type=_F32)
            h = jax.nn.silu(hg) * hu
            h_scr[:, cols] = h.astype(_BF16)
            produced.append(h)
            result = hg
        else:
            result = jnp.dot(h_scr[...], wd_ref[:, off:off + MXU_WIDTH],
                             preferred_element_type=_F32)
            parts.append(result)
        if s + PIN_LAG < len(steps):
            gate = _zero_tile([gate_source[0:SUBLANES, 0:LANES]])
            produced += filler.emit(
                _UP_CHUNK_WEIGHT if kind == "up" else _DOWN_CHUNK_WEIGHT, gate)
            pending.setdefault(s + PIN_LAG, []).extend(produced)
        gate_source = result
    y = jnp.concatenate(parts, axis=-1)
    return alpha * x + MACARON_WEIGHT * y


def _kv_kernel(mem_ref, g_ref, b_ref, wkt_ref, wv_ref, kt_ref, vb_ref, *, scale):
    memn = _layer_norm(mem_ref[0], g_ref[...], b_ref[...]).astype(_BF16)
    kt = lax.dot_general(wkt_ref[...], memn, (((1,), (1,)), ((), ())),
                         preferred_element_type=_F32) * scale
    v = jnp.dot(memn, wv_ref[...], preferred_element_type=_F32)
    n_mem, width = v.shape
    head_dim = width // N_XATTN_HEADS
    feat_row = lax.broadcasted_iota(jnp.int32, (width, n_mem), 0)
    feat_col = lax.broadcasted_iota(jnp.int32, (n_mem, width), 1)
    for h in range(N_XATTN_HEADS):
        lo, hi = h * head_dim, (h + 1) * head_dim
        in_row = (feat_row >= lo) & (feat_row < hi)
        in_col = (feat_col >= lo) & (feat_col < hi)
        kt_ref[0, :, h * n_mem:(h + 1) * n_mem] = jnp.where(in_row, kt, 0.0).astype(_BF16)
        vb_ref[0, h * n_mem:(h + 1) * n_mem, :] = jnp.where(in_col, v, 0.0).astype(_BF16)


def _ffn_in_kernel(x_ref, wg_ref, wu_ref, wd_ref, g_ref, b_ref, win_ref,
                   x1_ref, z_ref, r_scr, h_scr, *, alpha):
    @pl.when(pl.program_id(0) == 0)
    def _():
        _zero_fill(r_scr)

    def previous_tile(first_result):
        zero_row = jnp.concatenate(
            [_zero_tile([first_result[0:SUBLANES, 0:LANES]])[0:1, :]] * (g_ref.shape[1] // LANES),
            axis=1)
        x1 = _layer_norm(r_scr[...], g_ref[...] + zero_row, b_ref[...])
        x1_ref[0] = x1
        z_ref[0] = jnp.dot(x1.astype(_BF16), win_ref[...], preferred_element_type=_F32)

    r = _swiglu_residual(x_ref[0], wg_ref, wu_ref, wd_ref, h_scr, alpha, previous_tile)
    r_scr[...] = r


def _pool_pieces(ubuf, m_scr, t0, seq_len, rows):
    n_groups = len(POOL_WINDOWS)
    group = ubuf.shape[0] * LANES // n_groups
    pieces = []
    for j in range(ubuf.shape[0]):
        groups = [g for g in range(n_groups)
                  if g * group < (j + 1) * LANES and (g + 1) * group > j * LANES]

        def piece(gate, j=j, groups=groups, r0=0):
            if gate is not None:
                _add_into_tile(ubuf, (j, slice(HALO + r0, HALO + r0 + SUBLANES), slice(None)), gate)
            lane = j * LANES + lax.broadcasted_iota(jnp.int32, (MIX_ROWS, LANES), 1)
            row = lax.broadcasted_iota(jnp.int32, (MIX_ROWS, LANES), 0)
            in_group = [lane < (g + 1) * group for g in groups[:-1]]
            u0 = ubuf[j, pl.ds(HALO + r0, MIX_ROWS), :]
            sums = {}
            acc = None
            lo_prev, hi_prev = 0, -1
            for g in range(groups[-1] + 1):
                w = POOL_WINDOWS[g]
                lo, hi = -(w // 2), w - 1 - w // 2
                for d in list(range(lo, lo_prev)) + list(range(hi_prev + 1, hi + 1)):
                    term = u0 if d == 0 else ubuf[j, pl.ds(HALO + r0 + d, MIX_ROWS), :]
                    acc = term if acc is None else acc + term
                lo_prev, hi_prev = lo, hi
                sums[g] = acc
            win_sum = sums[groups[-1]]
            left = jnp.full((MIX_ROWS, LANES), POOL_WINDOWS[groups[-1]] // 2, jnp.int32)
            for g, mask in reversed(list(zip(groups[:-1], in_group))):
                win_sum = jnp.where(mask, sums[g], win_sum)
                left = jnp.where(mask, POOL_WINDOWS[g] // 2, left)
            t = t0 + r0 + row
            count = jnp.minimum(t + left, seq_len) - jnp.maximum(t - left, 0)
            m = win_sum / count.astype(_F32) - u0
            m_scr[r0:r0 + MIX_ROWS, j * LANES:(j + 1) * LANES] = m.astype(_BF16)
            return m

        cost = (MIX_ROWS // SUBLANES) * (POOL_WINDOWS[groups[-1]] + 24)
        for r0 in range(0, rows, MIX_ROWS):
            pieces.append((cost, functools.partial(piece, r0=r0)))
    return pieces


def _conv_pieces(hbuf, dww_ref, dwb_ref, out_scr, rows):
    taps = dww_ref.shape[0]
    base = HALO - taps // 2

    def piece(j, r0, gate):
        if gate is not None:
            _add_into_tile(hbuf, (j, slice(HALO + r0, HALO + r0 + SUBLANES), slice(None)), gate)
        lanes = slice(j * LANES, (j + 1) * LANES)
        w = dww_ref[:, lanes]
        acc = jnp.broadcast_to(dwb_ref[:, lanes], (MIX_ROWS, LANES))
        for k in range(taps):
            acc = acc + hbuf[j, pl.ds(r0 + base + k, MIX_ROWS), :] * w[k:k + 1, :]
        out_scr[r0:r0 + MIX_ROWS, lanes] = acc

    cost = (MIX_ROWS // SUBLANES) * 2 * taps
    return {r0: [(cost, functools.partial(piece, j, r0)) for j in range(hbuf.shape[0])]
            for r0 in range(0, rows, MIX_ROWS)}


def _mixer_pieces(z_ref, zp_ref, zn_ref, kt_ref, vb_ref, poolw_ref, pscale_ref, dww_ref, dwb_ref,
                  clng_ref, clnb_ref, ubuf, hbuf, conv_scr, m_scr, s_scr, p_scr, mix_scr,
                  i, n_tiles, seq_len):
    rows = z_ref.shape[1]
    pw = poolw_ref.shape[0]
    cw = dww_ref.shape[1]
    n_mem = vb_ref.shape[1] // N_XATTN_HEADS
    first = i == 0
    last = i == n_tiles - 1
    row_vregs = rows // SUBLANES
    pieces = []

    def glu_block(j, gate):
        a_cols = slice(pw + j * LANES, pw + (j + 1) * LANES)
        g_cols = slice(pw + cw + j * LANES, pw + cw + (j + 1) * LANES)
        glu = lambda ref: ref[0, :, a_cols] * jax.nn.sigmoid(ref[0, :, g_cols])
        hbuf[j, 0:HALO, :] = jnp.where(first, 0.0, glu(zp_ref))
        hbuf[j, HALO:HALO + rows, :] = glu(z_ref)
        hbuf[j, HALO + rows:, :] = jnp.where(last, 0.0, glu(zn_ref))

    def pool_block(j, gate):
        cols = slice(j * LANES, (j + 1) * LANES)
        ubuf[j, 0:HALO, :] = jnp.where(first, 0.0, zp_ref[0, :, cols])
        ubuf[j, HALO:HALO + rows, :] = z_ref[0, :, cols]
        ubuf[j, HALO + rows:, :] = jnp.where(last, 0.0, zn_ref[0, :, cols])

    def scores(gate):
        q = z_ref[0, :, pw + 2 * cw:].astype(_BF16)
        s_scr[...] = jnp.dot(q, kt_ref[0], preferred_element_type=_F32)

    def softmax(h, r0, gate):
        if gate is not None:
            _add_into_tile(s_scr, (slice(r0, r0 + SUBLANES),
                                  slice(h * n_mem, h * n_mem + LANES)), gate)
        sh = s_scr[r0:r0 + NORM_ROWS, h * n_mem:(h + 1) * n_mem]
        e = jnp.exp(sh - jnp.max(sh, axis=-1, keepdims=True))
        p = e / jnp.sum(e, axis=-1, keepdims=True)
        p_scr[r0:r0 + NORM_ROWS, h * n_mem:(h + 1) * n_mem] = p.astype(_BF16)
        return p

    def attn_out(gate):
        y = jnp.dot(p_scr[...], vb_ref[0], preferred_element_type=_F32)
        mix_scr[:, pw + cw:] = y.astype(_BF16)
        return y

    def pool_out(gate):
        y = jnp.dot(m_scr[...], poolw_ref[...], preferred_element_type=_F32)
        y = y * pscale_ref[...]
        mix_scr[:, 0:pw] = y.astype(_BF16)
        return y

    def conv_out(r0, gate):
        c = conv_scr[r0:r0 + MIX_ROWS, :]
        y = jax.nn.silu(_layer_norm(c, clng_ref[...], clnb_ref[...]))
        mix_scr[r0:r0 + MIX_ROWS, pw:pw + cw] = y.astype(_BF16)
        return y

    pieces += [(row_vregs, functools.partial(pool_block, j)) for j in range(pw // LANES)]
    pieces.append((row_vregs * 2, scores))
    pieces += [(row_vregs * 8, functools.partial(glu_block, j)) for j in range(cw // LANES)]
    pieces += _pool_pieces(ubuf, m_scr, i * rows, seq_len, rows)
    pieces.append((row_vregs * 4, pool_out))
    pieces += [((NORM_ROWS // SUBLANES) * (n_mem // LANES) * 8, functools.partial(softmax, h, r0))
               for h in range(N_XATTN_HEADS) for r0 in range(0, rows, NORM_ROWS)]
    pieces.append((row_vregs * 2, attn_out))
    for r0, conv_pieces in _conv_pieces(hbuf, dww_ref, dwb_ref, conv_scr, rows).items():
        pieces += conv_pieces
        pieces.append(((MIX_ROWS // SUBLANES) * (cw // LANES) * 16,
                       functools.partial(conv_out, r0)))
    return pieces


def _mix_ffn_kernel(x1_ref, z_ref, zp_ref, zn_ref, kt_ref, vb_ref,
                    poolw_ref, pscale_ref, dww_ref, dwb_ref, clng_ref, clnb_ref,
                    wout_ref, ln2g_ref, ln2b_ref, wg_ref, wu_ref, wd_ref, ln3g_ref, ln3b_ref,
                    o_ref, ubuf, hbuf, conv_scr, m_scr, s_scr, p_scr, mix_scr, xb_scr, h_scr, r_scr,
                    *, alpha, seq_len, n_tiles, n_total):
    t = pl.program_id(0)

    @pl.when(t == 0)
    def _():
        _zero_fill(mix_scr)
        _zero_fill(r_scr)

    def final_ln(r0, gate):
        y = _layer_norm(r_scr[r0:r0 + MIX_ROWS, :], ln3g_ref[...], ln3b_ref[...])
        o_ref[0, r0:r0 + MIX_ROWS, :] = y
        return y

    rows = z_ref.shape[1]
    i = lax.rem(jnp.minimum(t, n_total - 1), n_tiles)
    d_ff, d_model = wd_ref.shape
    ln_cost = (MIX_ROWS // SUBLANES) * (d_model // LANES) * 10
    filler = _Filler(
        [(ln_cost, functools.partial(final_ln, r0)) for r0 in range(0, rows, MIX_ROWS)]
        + _mixer_pieces(z_ref, zp_ref, zn_ref, kt_ref, vb_ref, poolw_ref, pscale_ref, dww_ref,
                        dwb_ref, clng_ref, clnb_ref, ubuf, hbuf, conv_scr, m_scr, s_scr, p_scr,
                        mix_scr, i, n_tiles, seq_len),
        total_weight=_filler_weight(d_ff, d_model))

    mix = jnp.dot(mix_scr[...], wout_ref[...], preferred_element_type=_F32)
    head = filler.emit(_OUT_PROJ_WEIGHT, None)
    alpha_anchored = alpha + _zero_tile(head)[0:1, 0:1]
    x2 = _layer_norm(alpha_anchored * x1_ref[0] + mix, ln2g_ref[...], ln2b_ref[...])
    xb_scr[...] = x2.astype(_BF16)
    r = _swiglu_residual_filled(x2, wg_ref, wu_ref, wd_ref, xb_scr, h_scr, alpha, filler)
    assert filler.done()
    r_scr[...] = r


def _const_spec(shape):
    zeros = (0,) * len(shape)
    return pl.BlockSpec(shape, lambda *_: zeros, pipeline_mode=pl.Buffered(1))


def _compiler_params(n_axes):
    return pltpu.CompilerParams(dimension_semantics=("arbitrary",) * n_axes,
                                vmem_limit_bytes=VMEM_LIMIT_BYTES)


def _row(v):
    return v.reshape(1, -1)


def _prepare_layer(p):
    bf = lambda w: w.astype(_BF16)
    aw = p["w_kv"].shape[1] // 2
    pool_w = p["pool_w"]
    q = {k: bf(p[k]) for k in ("ffn1_w_gate", "ffn1_w_up", "ffn1_w_down", "w_in", "w_out",
                               "ffn2_w_gate", "ffn2_w_up", "ffn2_w_down")}
    q.update({k: _row(p[k]) for k in ("ln1_g", "ln1_b", "pool_scale", "conv_dw_b", "conv_ln_g",
                                      "conv_ln_b", "mem_ln_g", "mem_ln_b", "ln2_g", "ln2_b",
                                      "ln3_g", "ln3_b")})
    q["conv_dw_w"] = p["conv_dw_w"]
    q["w_k_t"] = bf(p["w_kv"][:, :aw].T)
    q["w_v"] = bf(p["w_kv"][:, aw:])
    q["pool_w_bd"] = bf(jax.scipy.linalg.block_diag(*[pool_w[g] for g in range(pool_w.shape[0])]))
    return q


def _encoder_layer(x, mem, p, alpha):
    batch, seq_len, d_model = x.shape
    n_mem = mem.shape[1]
    d_ff = p["ffn1_w_gate"].shape[1]
    in_width = p["w_in"].shape[1]
    pw = p["pool_scale"].shape[1]
    cw = p["conv_dw_w"].shape[1]
    aw = p["w_v"].shape[1]
    rows = ROW_TILE
    assert seq_len % rows == 0 and rows % HALO == 0
    assert rows % MIX_ROWS == 0 and rows % NORM_ROWS == 0
    assert d_ff % FF_CHUNK == 0 and d_model % MXU_WIDTH == 0
    assert pw % LANES == 0 and cw % LANES == 0
    assert n_mem % LANES == 0 and aw % N_XATTN_HEADS == 0
    assert in_width == pw + 2 * cw + aw and pw + cw + aw == p["w_out"].shape[0]
    assert p["conv_dw_w"].shape[0] // 2 < HALO and POOL_WINDOWS[-1] // 2 <= HALO
    n_tiles = seq_len // rows
    n_total = batch * n_tiles
    halo_per_tile = rows // HALO
    n_halo_blocks = seq_len // HALO

    kt, vb = pl.pallas_call(
        functools.partial(_kv_kernel, scale=1.0 / math.sqrt(aw // N_XATTN_HEADS)),
        grid=(batch,),
        in_specs=[pl.BlockSpec((1, n_mem, d_model), lambda b: (b, 0, 0)),
                  _const_spec((1, d_model)), _const_spec((1, d_model)),
                  _const_spec((aw, d_model)), _const_spec((d_model, aw))],
        out_specs=[pl.BlockSpec((1, aw, N_XATTN_HEADS * n_mem), lambda b: (b, 0, 0)),
                   pl.BlockSpec((1, N_XATTN_HEADS * n_mem, aw), lambda b: (b, 0, 0))],
        out_shape=[jax.ShapeDtypeStruct((batch, aw, N_XATTN_HEADS * n_mem), _BF16),
                   jax.ShapeDtypeStruct((batch, N_XATTN_HEADS * n_mem, aw), _BF16)],
        compiler_params=_compiler_params(1),
        name="kv",
    )(mem, p["mem_ln_g"], p["mem_ln_b"], p["w_k_t"], p["w_v"])

    def staged_tile(t, lag):
        tile_id = jnp.clip(t - lag, 0, n_total - 1)
        return tile_id // n_tiles, tile_id % n_tiles

    def staged(width, lag):
        return pl.BlockSpec((1, rows, width), lambda t: (*staged_tile(t, lag), 0))

    x1, z = pl.pallas_call(
        functools.partial(_ffn_in_kernel, alpha=alpha),
        grid=(n_total + 1,),
        in_specs=[staged(d_model, 0),
                  _const_spec((d_model, d_ff)), _const_spec((d_model, d_ff)),
                  _const_spec((d_ff, d_model)),
                  _const_spec((1, d_model)), _const_spec((1, d_model)),
                  _const_spec((d_model, in_width))],
        out_specs=[staged(d_model, 1), staged(in_width, 1)],
        out_shape=[jax.ShapeDtypeStruct((batch, seq_len, d_model), _F32),
                   jax.ShapeDtypeStruct((batch, seq_len, in_width), _F32)],
        scratch_shapes=[pltpu.VMEM((rows, d_model), _F32),
                        pltpu.VMEM((rows, d_ff), _BF16)],
        compiler_params=_compiler_params(1),
        name="ffn_in",
    )(x, p["ffn1_w_gate"], p["ffn1_w_up"], p["ffn1_w_down"], p["ln1_g"], p["ln1_b"], p["w_in"])

    def halo_prev_map(t):
        b, i = staged_tile(t, 0)
        return b, jnp.maximum(i * halo_per_tile - 1, 0), 0

    def halo_next_map(t):
        b, i = staged_tile(t, 0)
        return b, jnp.minimum((i + 1) * halo_per_tile, n_halo_blocks - 1), 0

    per_batch = lambda s0, s1: pl.BlockSpec((1, s0, s1), lambda t: (staged_tile(t, 0)[0], 0, 0))
    return pl.pallas_call(
        functools.partial(_mix_ffn_kernel, alpha=alpha, seq_len=seq_len,
                          n_tiles=n_tiles, n_total=n_total),
        grid=(n_total + 2,),
        in_specs=[staged(d_model, 1),
                  staged(in_width, 0),
                  pl.BlockSpec((1, HALO, in_width), halo_prev_map),
                  pl.BlockSpec((1, HALO, in_width), halo_next_map),
                  per_batch(aw, N_XATTN_HEADS * n_mem), per_batch(N_XATTN_HEADS * n_mem, aw),
                  _const_spec((pw, pw)), _const_spec((1, pw)),
                  _const_spec(p["conv_dw_w"].shape), _const_spec((1, cw)),
                  _const_spec((1, cw)), _const_spec((1, cw)),
                  _const_spec((pw + cw + aw, d_model)),
                  _const_spec((1, d_model)), _const_spec((1, d_model)),
                  _const_spec((d_model, d_ff)), _const_spec((d_model, d_ff)),
                  _const_spec((d_ff, d_model)),
                  _const_spec((1, d_model)), _const_spec((1, d_model))],
        out_specs=staged(d_model, 2),
        out_shape=jax.ShapeDtypeStruct((batch, seq_len, d_model), _F32),
        scratch_shapes=[pltpu.VMEM((pw // LANES, rows + 2 * HALO, LANES), _F32),
                        pltpu.VMEM((cw // LANES, rows + 2 * HALO, LANES), _F32),
                        pltpu.VMEM((rows, cw), _F32),
                        pltpu.VMEM((rows, pw), _BF16),
                        pltpu.VMEM((rows, N_XATTN_HEADS * n_mem), _F32),
                        pltpu.VMEM((rows, N_XATTN_HEADS * n_mem), _BF16),
                        pltpu.VMEM((rows, pw + cw + aw), _BF16),
                        pltpu.VMEM((rows, d_model), _BF16),
                        pltpu.VMEM((rows, d_ff), _BF16),
                        pltpu.VMEM((rows, d_model), _F32)],
        compiler_params=_compiler_params(1),
        name="mix",
    )(x1, z, z, z, kt, vb,
      p["pool_w_bd"], p["pool_scale"], p["conv_dw_w"], p["conv_dw_b"],
      p["conv_ln_g"], p["conv_ln_b"], p["w_out"], p["ln2_g"], p["ln2_b"],
      p["ffn2_w_gate"], p["ffn2_w_up"], p["ffn2_w_down"], p["ln3_g"], p["ln3_b"])


_PARAM_NAMES = (
    "ffn1_w_gate", "ffn1_w_up", "ffn1_w_down", "ln1_g", "ln1_b",
    "w_in", "pool_w", "pool_scale", "conv_dw_w", "conv_dw_b", "conv_ln_g", "conv_ln_b",
    "mem_ln_g", "mem_ln_b", "w_kv", "w_out", "ln2_g", "ln2_b",
    "ffn2_w_gate", "ffn2_w_up", "ffn2_w_down", "ln3_g", "ln3_b")


def kernel(x_prompt, x_sample, mem_prompt, mem_sample, ffn1_w_gate, ffn1_w_up, ffn1_w_down, ln1_g, ln1_b, w_in, pool_w, pool_scale, conv_dw_w, conv_dw_b, conv_ln_g, conv_ln_b, mem_ln_g, mem_ln_b, w_kv, w_out, ln2_g, ln2_b, ffn2_w_gate, ffn2_w_up, ffn2_w_down, ln3_g, ln3_b):
    stacked = dict(zip(_PARAM_NAMES, (
        ffn1_w_gate, ffn1_w_up, ffn1_w_down, ln1_g, ln1_b,
        w_in, pool_w, pool_scale, conv_dw_w, conv_dw_b, conv_ln_g, conv_ln_b,
        mem_ln_g, mem_ln_b, w_kv, w_out, ln2_g, ln2_b,
        ffn2_w_gate, ffn2_w_up, ffn2_w_down, ln3_g, ln3_b)))
    depth = ffn1_w_gate.shape[0]
    alpha = (2.0 * depth) ** 0.25
    layers = [_prepare_layer({k: v[layer] for k, v in stacked.items()}) for layer in range(depth)]

    def trunk(x, mem):
        for p in layers:
            x = _encoder_layer(x, mem, p, alpha)
        return x

    return (trunk(x_prompt, mem_prompt), trunk(x_sample, mem_sample))
```

```python
import functools
import math

import jax
import jax.numpy as jnp
from jax import lax
from jax.experimental import pallas as pl
from jax.experimental.pallas import tpu as pltpu

LN_EPS = 1e-5
MACARON_WEIGHT = 0.5
POOL_WINDOWS = (2, 4, 8, 16)
N_XATTN_HEADS = 4

LANES = 128
SUBLANES = 8
MXU_WIDTH = 256
HALO = 16
ROW_TILE = 512
FF_CHUNK = 256
MIX_ROWS = 64
NORM_ROWS = 128
PIN_LAG = 4
VMEM_LIMIT_BYTES = 56 * 1024 * 1024

_BF16 = jnp.bfloat16
_F32 = jnp.float32


def _layer_norm(r, g, b):
    mu = jnp.mean(r, axis=-1, keepdims=True)
    rc = r - mu
    var = jnp.mean(rc * rc, axis=-1, keepdims=True)
    return rc * lax.rsqrt(var + LN_EPS) * g + b


def _zero_tile(values):
    word = None
    for v in values:
        bits = pltpu.bitcast(v, jnp.uint32)
        for r in range(0, bits.shape[0], SUBLANES):
            for c in range(0, bits.shape[1], LANES):
                blk = bits[r:r + SUBLANES, c:c + LANES]
                word = blk if word is None else word | blk
    sixteen = jnp.uint32(16)
    word = lax.shift_right_logical(lax.shift_right_logical(word, sixteen), sixteen)
    return pltpu.bitcast(word, _F32)


def _add_into_tile(ref, index, zero):
    tile = ref[index].astype(_F32)
    zero = jnp.concatenate([zero] * (tile.shape[0] // SUBLANES), axis=0)
    ref[index] = (tile + zero).astype(ref.dtype)


def _first_tile(ref):
    rows = SUBLANES * 4 // ref.dtype.itemsize
    return (slice(0, rows), slice(0, LANES))


def _zero_fill(ref):
    rows = 8 * SUBLANES

    def body(k, carry):
        start = pl.multiple_of(k * rows, rows)
        ref[pl.ds(start, rows), :] = jnp.zeros((rows, ref.shape[1]), ref.dtype)
        return carry

    lax.fori_loop(0, ref.shape[0] // rows, body, 0)


class _Filler:
    def __init__(self, pieces, total_weight):
        self._pieces = list(pieces)
        self._total_cost = sum(c for c, _ in self._pieces)
        self._total_weight = total_weight
        self._weight = 0.0
        self._cost = 0.0

    def emit(self, weight, gate):
        self._weight += weight
        target = self._total_cost * min(self._weight / self._total_weight, 1.0)
        produced = []
        while self._pieces and self._cost + 0.5 * self._pieces[0][0] <= target:
            cost, thunk = self._pieces.pop(0)
            value = thunk(gate)
            if value is not None:
                produced.append(value)
            self._cost += cost
        return produced

    def done(self):
        return not self._pieces


def _swiglu_residual(x, wg_ref, wu_ref, wd_ref, h_scr, alpha, after_first_chunk=None):
    xb = x.astype(_BF16)
    d_ff = wg_ref.shape[1]
    for c0 in range(0, d_ff, FF_CHUNK):
        cols = slice(c0, c0 + FF_CHUNK)
        hg = jnp.dot(xb, wg_ref[:, cols], preferred_element_type=_F32)
        hu = jnp.dot(xb, wu_ref[:, cols], preferred_element_type=_F32)
        h_scr[:, cols] = (jax.nn.silu(hg) * hu).astype(_BF16)
        if c0 == 0 and after_first_chunk is not None:
            after_first_chunk(hg)
    y = jnp.dot(h_scr[...], wd_ref[...], preferred_element_type=_F32)
    return alpha * x + MACARON_WEIGHT * y


_UP_CHUNK_WEIGHT = 1.0
_DOWN_CHUNK_WEIGHT = 1.375
_OUT_PROJ_WEIGHT = 2.5


def _ffn_steps(d_ff, d_model):
    return ([("up", c0) for c0 in range(0, d_ff, FF_CHUNK)]
            + [("down", n0) for n0 in range(0, d_model, MXU_WIDTH)])


def _filler_weight(d_ff, d_model):
    steps = _ffn_steps(d_ff, d_model)[:-PIN_LAG]
    return _OUT_PROJ_WEIGHT + sum(
        _UP_CHUNK_WEIGHT if kind == "up" else _DOWN_CHUNK_WEIGHT for kind, _ in steps)


def _swiglu_residual_filled(x, wg_ref, wu_ref, wd_ref, xb_scr, h_scr, alpha, filler):
    d_ff, d_model = wd_ref.shape
    steps = _ffn_steps(d_ff, d_model)
    pending = {}
    gate_source = x
    parts = []
    for s, (kind, off) in enumerate(steps):
        lhs_ref = xb_scr if kind == "up" else h_scr
        if pending.get(s):
            _add_into_tile(lhs_ref, _first_tile(lhs_ref), _zero_tile(pending.pop(s)))
        produced = []
        if kind == "up":
            cols = slice(off, off + FF_CHUNK)
            hg = jnp.dot(xb_scr[...], wg_ref[:, cols], preferred_element_type=_F32)
            hu = jnp.dot(xb_scr[...], wu_ref[:, cols], preferred_element_type=_F32)
            h = jax.nn.silu(hg) * hu
            h_scr[:, cols] = h.astype(_BF16)
            produced.append(h)
            result = hg
        else:
            result = jnp.dot(h_scr[...], wd_ref[:, off:off + MXU_WIDTH],
                             preferred_element_type=_F32)
            parts.append(result)
        if s + PIN_LAG < len(steps):
            gate = _zero_tile([gate_source[0:SUBLANES, 0:LANES]])
            produced += filler.emit(
                _UP_CHUNK_WEIGHT if kind == "up" else _DOWN_CHUNK_WEIGHT, gate)
            pending.setdefault(s + PIN_LAG, []).extend(produced)
        gate_source = result
    y = jnp.concatenate(parts, axis=-1)
    return alpha * x + MACARON_WEIGHT * y


def _kv_kernel(mem_ref, g_ref, b_ref, wkt_ref, wv_ref, kt_ref, vb_ref, *, scale):
    memn = _layer_norm(mem_ref[0], g_ref[...], b_ref[...]).astype(_BF16)
    kt = lax.dot_general(wkt_ref[...], memn, (((1,), (1,)), ((), ())),
                         preferred_element_type=_F32) * scale
    v = jnp.dot(memn, wv_ref[...], preferred_element_type=_F32)
    n_mem, width = v.shape
    head_dim = width // N_XATTN_HEADS
    feat_row = lax.broadcasted_iota(jnp.int32, (width, n_mem), 0)
    feat_col = lax.broadcasted_iota(jnp.int32, (n_mem, width), 1)
    for h in range(N_XATTN_HEADS):
        lo, hi = h * head_dim, (h + 1) * head_dim
        in_row = (feat_row >= lo) & (feat_row < hi)
        in_col = (feat_col >= lo) & (feat_col < hi)
        kt_ref[0, :, h * n_mem:(h + 1) * n_mem] = jnp.where(in_row, kt, 0.0).astype(_BF16)
        vb_ref[0, h * n_mem:(h + 1) * n_mem, :] = jnp.where(in_col, v, 0.0).astype(_BF16)


def _ffn_in_kernel(x_ref, wg_ref, wu_ref, wd_ref, g_ref, b_ref, win_ref,
                   x1_ref, zc_ref, q_ref, r_scr, h_scr, *, alpha, pool_width, conv_width):
    @pl.when(pl.program_id(0) == 0)
    def _():
        _zero_fill(r_scr)

    def previous_tile(first_result):
        zero_row = jnp.concatenate(
            [_zero_tile([first_result[0:SUBLANES, 0:LANES]])[0:1, :]] * (g_ref.shape[1] // LANES),
            axis=1)
        x1 = _layer_norm(r_scr[...], g_ref[...] + zero_row, b_ref[...])
        x1_ref[0] = x1
        z = jnp.dot(x1.astype(_BF16), win_ref[...], preferred_element_type=_F32)
        pw, cw = pool_width, conv_width
        zc_ref[0, :, 0:pw] = z[:, 0:pw]
        zc_ref[0, :, pw:] = z[:, pw:pw + cw] * jax.nn.sigmoid(z[:, pw + cw:pw + 2 * cw])
        q_ref[0] = z[:, pw + 2 * cw:].astype(_BF16)

    r = _swiglu_residual(x_ref[0], wg_ref, wu_ref, wd_ref, h_scr, alpha, previous_tile)
    r_scr[...] = r


def _pool_pieces(ubuf, m_scr, t0, seq_len, rows):
    n_groups = len(POOL_WINDOWS)
    group = ubuf.shape[0] * LANES // n_groups
    pieces = []
    for j in range(ubuf.shape[0]):
        groups = [g for g in range(n_groups)
                  if g * group < (j + 1) * LANES and (g + 1) * group > j * LANES]

        def piece(gate, j=j, groups=groups, r0=0):
            if gate is not None:
                _add_into_tile(ubuf, (j, slice(HALO + r0, HALO + r0 + SUBLANES), slice(None)), gate)
            lane = j * LANES + lax.broadcasted_iota(jnp.int32, (MIX_ROWS, LANES), 1)
            row = lax.broadcasted_iota(jnp.int32, (MIX_ROWS, LANES), 0)
            in_group = [lane < (g + 1) * group for g in groups[:-1]]
            u0 = ubuf[j, pl.ds(HALO + r0, MIX_ROWS), :]
            sums = {}
            acc = None
            lo_prev, hi_prev = 0, -1
            for g in range(groups[-1] + 1):
                w = POOL_WINDOWS[g]
                lo, hi = -(w // 2), w - 1 - w // 2
                for d in list(range(lo, lo_prev)) + list(range(hi_prev + 1, hi + 1)):
                    term = u0 if d == 0 else ubuf[j, pl.ds(HALO + r0 + d, MIX_ROWS), :]
                    acc = term if acc is None else acc + term
                lo_prev, hi_prev = lo, hi
                sums[g] = acc
            win_sum = sums[groups[-1]]
            left = jnp.full((MIX_ROWS, LANES), POOL_WINDOWS[groups[-1]] // 2, jnp.int32)
            for g, mask in reversed(list(zip(groups[:-1], in_group))):
                win_sum = jnp.where(mask, sums[g], win_sum)
                left = jnp.where(mask, POOL_WINDOWS[g] // 2, left)
            t = t0 + r0 + row
            count = jnp.minimum(t + left, seq_len) - jnp.maximum(t - left, 0)
            m = win_sum / count.astype(_F32) - u0
            m_scr[r0:r0 + MIX_ROWS, j * LANES:(j + 1) * LANES] = m.astype(_BF16)
            return m

        cost = (MIX_ROWS // SUBLANES) * (POOL_WINDOWS[groups[-1]] + 24)
        for r0 in range(0, rows, MIX_ROWS):
            pieces.append((cost, functools.partial(piece, r0=r0)))
    return pieces


def _conv_pieces(hbuf, dww_ref, dwb_ref, out_scr, rows):
    taps = dww_ref.shape[0]
    base = HALO - taps // 2

    def piece(j, r0, gate):
        if gate is not None:
            _add_into_tile(hbuf, (j, slice(HALO + r0, HALO + r0 + SUBLANES), slice(None)), gate)
        lanes = slice(j * LANES, (j + 1) * LANES)
        w = dww_ref[:, lanes]
        acc = jnp.broadcast_to(dwb_ref[:, lanes], (MIX_ROWS, LANES))
        for k in range(taps):
            acc = acc + hbuf[j, pl.ds(r0 + base + k, MIX_ROWS), :] * w[k:k + 1, :]
        out_scr[r0:r0 + MIX_ROWS, lanes] = acc

    cost = (MIX_ROWS // SUBLANES) * 2 * taps
    return {r0: [(cost, functools.partial(piece, j, r0)) for j in range(hbuf.shape[0])]
            for r0 in range(0, rows, MIX_ROWS)}


def _mixer_pieces(z_ref, zp_ref, zn_ref, q_ref, kt_ref, vb_ref, poolw_ref, pscale_ref, dww_ref, dwb_ref,
                  clng_ref, clnb_ref, ubuf, hbuf, conv_scr, m_scr, s_scr, p_scr, mix_scr,
                  i, n_tiles, seq_len):
    rows = z_ref.shape[1]
    pw = poolw_ref.shape[0]
    cw = dww_ref.shape[1]
    n_mem = vb_ref.shape[1] // N_XATTN_HEADS
    first = i == 0
    last = i == n_tiles - 1
    row_vregs = rows // SUBLANES
    pieces = []

    def halo_block(buf, j, c0, gate):
        cols = slice(c0 + j * LANES, c0 + (j + 1) * LANES)
        buf[j, 0:HALO, :] = jnp.where(first, 0.0, zp_ref[0, :, cols])
        buf[j, HALO:HALO + rows, :] = z_ref[0, :, cols]
        buf[j, HALO + rows:, :] = jnp.where(last, 0.0, zn_ref[0, :, cols])

    def scores(gate):
        s_scr[...] = jnp.dot(q_ref[0], kt_ref[0], preferred_element_type=_F32)

    def softmax(h, r0, gate):
        if gate is not None:
            _add_into_tile(s_scr, (slice(r0, r0 + SUBLANES),
                                  slice(h * n_mem, h * n_mem + LANES)), gate)
        sh = s_scr[r0:r0 + NORM_ROWS, h * n_mem:(h + 1) * n_mem]
        e = jnp.exp(sh - jnp.max(sh, axis=-1, keepdims=True))
        p = e / jnp.sum(e, axis=-1, keepdims=True)
        p_scr[r0:r0 + NORM_ROWS, h * n_mem:(h + 1) * n_mem] = p.astype(_BF16)
        return p

    def attn_out(gate):
        y = jnp.dot(p_scr[...], vb_ref[0], preferred_element_type=_F32)
        mix_scr[:, pw + cw:] = y.astype(_BF16)
        return y

    def pool_out(gate):
        y = jnp.dot(m_scr[...], poolw_ref[...], preferred_element_type=_F32)
        y = y * pscale_ref[...]
        mix_scr[:, 0:pw] = y.astype(_BF16)
        return y

    def conv_out(r0, gate):
        c = conv_scr[r0:r0 + MIX_ROWS, :]
        y = jax.nn.silu(_layer_norm(c, clng_ref[...], clnb_ref[...]))
        mix_scr[r0:r0 + MIX_ROWS, pw:pw + cw] = y.astype(_BF16)
        return y

    pieces += [(row_vregs, functools.partial(halo_block, ubuf, j, 0)) for j in range(pw // LANES)]
    pieces.append((row_vregs * 2, scores))
    pieces += [(row_vregs, functools.partial(halo_block, hbuf, j, pw)) for j in range(cw // LANES)]
    pieces += _pool_pieces(ubuf, m_scr, i * rows, seq_len, rows)
    pieces.append((row_vregs * 4, pool_out))
    pieces += [((NORM_ROWS // SUBLANES) * (n_mem // LANES) * 8, functools.partial(softmax, h, r0))
               for h in range(N_XATTN_HEADS) for r0 in range(0, rows, NORM_ROWS)]
    pieces.append((row_vregs * 2, attn_out))
    for r0, conv_pieces in _conv_pieces(hbuf, dww_ref, dwb_ref, conv_scr, rows).items():
        pieces += conv_pieces
        pieces.append(((MIX_ROWS // SUBLANES) * (cw // LANES) * 16,
                       functools.partial(conv_out, r0)))
    return pieces


def _mix_ffn_kernel(x1_ref, z_ref, zp_ref, zn_ref, q_ref, kt_ref, vb_ref,
                    poolw_ref, pscale_ref, dww_ref, dwb_ref, clng_ref, clnb_ref,
                    wout_ref, ln2g_ref, ln2b_ref, wg_ref, wu_ref, wd_ref, ln3g_ref, ln3b_ref,
                    o_ref, ubuf, hbuf, conv_scr, m_scr, s_scr, p_scr, mix_scr, xb_scr, h_scr, r_scr,
                    *, alpha, seq_len, n_tiles, n_total):
    t = pl.program_id(0)

    @pl.when(t == 0)
    def _():
        _zero_fill(mix_scr)
        _zero_fill(r_scr)

    def final_ln(r0, gate):
        y = _layer_norm(r_scr[r0:r0 + MIX_ROWS, :], ln3g_ref[...], ln3b_ref[...])
        o_ref[0, r0:r0 + MIX_ROWS, :] = y
        return y

    rows = z_ref.shape[1]
    i = lax.rem(jnp.minimum(t, n_total - 1), n_tiles)
    d_ff, d_model = wd_ref.shape
    ln_cost = (MIX_ROWS // SUBLANES) * (d_model // LANES) * 10
    filler = _Filler(
        [(ln_cost, functools.partial(final_ln, r0)) for r0 in range(0, rows, MIX_ROWS)]
        + _mixer_pieces(z_ref, zp_ref, zn_ref, q_ref, kt_ref, vb_ref, poolw_ref, pscale_ref, dww_ref,
                        dwb_ref, clng_ref, clnb_ref, ubuf, hbuf, conv_scr, m_scr, s_scr, p_scr,
                        mix_scr, i, n_tiles, seq_len),
        total_weight=_filler_weight(d_ff, d_model))

    mix = jnp.dot(mix_scr[...], wout_ref[...], preferred_element_type=_F32)
    head = filler.emit(_OUT_PROJ_WEIGHT, None)
    alpha_anchored = alpha + _zero_tile(head)[0:1, 0:1]
    x2 = _layer_norm(alpha_anchored * x1_ref[0] + mix, ln2g_ref[...], ln2b_ref[...])
    xb_scr[...] = x2.astype(_BF16)
    r = _swiglu_residual_filled(x2, wg_ref, wu_ref, wd_ref, xb_scr, h_scr, alpha, filler)
    assert filler.done()
    r_scr[...] = r


def _const_spec(shape):
    zeros = (0,) * len(shape)
    return pl.BlockSpec(shape, lambda *_: zeros, pipeline_mode=pl.Buffered(1))


def _compiler_params(n_axes):
    return pltpu.CompilerParams(dimension_semantics=("arbitrary",) * n_axes,
                                vmem_limit_bytes=VMEM_LIMIT_BYTES)


def _row(v):
    return v.reshape(1, -1)


def _prepare_layer(p):
    bf = lambda w: w.astype(_BF16)
    aw = p["w_kv"].shape[1] // 2
    pool_w = p["pool_w"]
    q = {k: bf(p[k]) for k in ("ffn1_w_gate", "ffn1_w_up", "ffn1_w_down", "w_in", "w_out",
                               "ffn2_w_gate", "ffn2_w_up", "ffn2_w_down")}
    q.update({k: _row(p[k]) for k in ("ln1_g", "ln1_b", "pool_scale", "conv_dw_b", "conv_ln_g",
                                      "conv_ln_b", "mem_ln_g", "mem_ln_b", "ln2_g", "ln2_b",
                                      "ln3_g", "ln3_b")})
    q["conv_dw_w"] = p["conv_dw_w"]
    q["w_k_t"] = bf(p["w_kv"][:, :aw].T)
    q["w_v"] = bf(p["w_kv"][:, aw:])
    q["pool_w_bd"] = bf(jax.scipy.linalg.block_diag(*[pool_w[g] for g in range(pool_w.shape[0])]))
    return q


def _encoder_layer(x, mem, p, alpha):
    batch, seq_len, d_model = x.shape
    n_mem = mem.shape[1]
    d_ff = p["ffn1_w_gate"].shape[1]
    in_width = p["w_in"].shape[1]
    pw = p["pool_scale"].shape[1]
    cw = p["conv_dw_w"].shape[1]
    aw = p["w_v"].shape[1]
    rows = ROW_TILE
    assert seq_len % rows == 0 and rows % HALO == 0
    assert rows % MIX_ROWS == 0 and rows % NORM_ROWS == 0
    assert d_ff % FF_CHUNK == 0 and d_model % MXU_WIDTH == 0
    assert pw % LANES == 0 and cw % LANES == 0
    assert n_mem % LANES == 0 and aw % N_XATTN_HEADS == 0
    assert in_width == pw + 2 * cw + aw and pw + cw + aw == p["w_out"].shape[0]
    assert p["conv_dw_w"].shape[0] // 2 < HALO and POOL_WINDOWS[-1] // 2 <= HALO
    n_tiles = seq_len // rows
    n_total = batch * n_tiles
    halo_per_tile = rows // HALO
    n_halo_blocks = seq_len // HALO

    kt, vb = pl.pallas_call(
        functools.partial(_kv_kernel, scale=1.0 / math.sqrt(aw // N_XATTN_HEADS)),
        grid=(batch,),
        in_specs=[pl.BlockSpec((1, n_mem, d_model), lambda b: (b, 0, 0)),
                  _const_spec((1, d_model)), _const_spec((1, d_model)),
                  _const_spec((aw, d_model)), _const_spec((d_model, aw))],
        out_specs=[pl.BlockSpec((1, aw, N_XATTN_HEADS * n_mem), lambda b: (b, 0, 0)),
                   pl.BlockSpec((1, N_XATTN_HEADS * n_mem, aw), lambda b: (b, 0, 0))],
        out_shape=[jax.ShapeDtypeStruct((batch, aw, N_XATTN_HEADS * n_mem), _BF16),
                   jax.ShapeDtypeStruct((batch, N_XATTN_HEADS * n_mem, aw), _BF16)],
        compiler_params=_compiler_params(1),
        name="kv",
    )(mem, p["mem_ln_g"], p["mem_ln_b"], p["w_k_t"], p["w_v"])

    def staged_tile(t, lag):
        tile_id = jnp.clip(t - lag, 0, n_total - 1)
        return tile_id // n_tiles, tile_id % n_tiles

    def staged(width, lag):
        return pl.BlockSpec((1, rows, width), lambda t: (*staged_tile(t, lag), 0))

    zc_width = pw + cw
    x1, zc, q = pl.pallas_call(
        functools.partial(_ffn_in_kernel, alpha=alpha, pool_width=pw, conv_width=cw),
        grid=(n_total + 1,),
        in_specs=[staged(d_model, 0),
                  _const_spec((d_model, d_ff)), _const_spec((d_model, d_ff)),
                  _const_spec((d_ff, d_model)),
                  _const_spec((1, d_model)), _const_spec((1, d_model)),
                  _const_spec((d_model, in_width))],
        out_specs=[staged(d_model, 1), staged(zc_width, 1), staged(aw, 1)],
        out_shape=[jax.ShapeDtypeStruct((batch, seq_len, d_model), _F32),
                   jax.ShapeDtypeStruct((batch, seq_len, zc_width), _F32),
                   jax.ShapeDtypeStruct((batch, seq_len, aw), _BF16)],
        scratch_shapes=[pltpu.VMEM((rows, d_model), _F32),
                        pltpu.VMEM((rows, d_ff), _BF16)],
        compiler_params=_compiler_params(1),
        name="ffn_in",
    )(x, p["ffn1_w_gate"], p["ffn1_w_up"], p["ffn1_w_down"], p["ln1_g"], p["ln1_b"], p["w_in"])

    def halo_prev_map(t):
        b, i = staged_tile(t, 0)
        return b, jnp.maximum(i * halo_per_tile - 1, 0), 0

    def halo_next_map(t):
        b, i = staged_tile(t, 0)
        return b, jnp.minimum((i + 1) * halo_per_tile, n_halo_blocks - 1), 0

    per_batch = lambda s0, s1: pl.BlockSpec((1, s0, s1), lambda t: (staged_tile(t, 0)[0], 0, 0))
    return pl.pallas_call(
        functools.partial(_mix_ffn_kernel, alpha=alpha, seq_len=seq_len,
                          n_tiles=n_tiles, n_total=n_total),
        grid=(n_total + 2,),
        in_specs=[staged(d_model, 1),
                  staged(zc_width, 0),
                  pl.BlockSpec((1, HALO, zc_width), halo_prev_map),
                  pl.BlockSpec((1, HALO, zc_width), halo_next_map),
                  staged(aw, 0),
                  per_batch(aw, N_XATTN_HEADS * n_mem), per_batch(N_XATTN_HEADS * n_mem, aw),
                  _const_spec((pw, pw)), _const_spec((1, pw)),
                  _const_spec(p["conv_dw_w"].shape), _const_spec((1, cw)),
                  _const_spec((1, cw)), _const_spec((1, cw)),
                  _const_spec((pw + cw + aw, d_model)),
                  _const_spec((1, d_model)), _const_spec((1, d_model)),
                  _const_spec((d_model, d_ff)), _const_spec((d_model, d_ff)),
                  _const_spec((d_ff, d_model)),
                  _const_spec((1, d_model)), _const_spec((1, d_model))],
        out_specs=staged(d_model, 2),
        out_shape=jax.ShapeDtypeStruct((batch, seq_len, d_model), _F32),
        scratch_shapes=[pltpu.VMEM((pw // LANES, rows + 2 * HALO, LANES), _F32),
                        pltpu.VMEM((cw // LANES, rows + 2 * HALO, LANES), _F32),
                        pltpu.VMEM((rows, cw), _F32),
                        pltpu.VMEM((rows, pw), _BF16),
                        pltpu.VMEM((rows, N_XATTN_HEADS * n_mem), _F32),
                        pltpu.VMEM((rows, N_XATTN_HEADS * n_mem), _BF16),
                        pltpu.VMEM((rows, pw + cw + aw), _BF16),
                        pltpu.VMEM((rows, d_model), _BF16),
                        pltpu.VMEM((rows, d_ff), _BF16),
                        pltpu.VMEM((rows, d_model), _F32)],
        compiler_params=_compiler_params(1),
        name="mix",
    )(x1, zc, zc, zc, q, kt, vb,
      p["pool_w_bd"], p["pool_scale"], p["conv_dw_w"], p["conv_dw_b"],
      p["conv_ln_g"], p["conv_ln_b"], p["w_out"], p["ln2_g"], p["ln2_b"],
      p["ffn2_w_gate"], p["ffn2_w_up"], p["ffn2_w_down"], p["ln3_g"], p["ln3_b"])


_PARAM_NAMES = (
    "ffn1_w_gate", "ffn1_w_up", "ffn1_w_down", "ln1_g", "ln1_b",
    "w_in", "pool_w", "pool_scale", "conv_dw_w", "conv_dw_b", "conv_ln_g", "conv_ln_b",
    "mem_ln_g", "mem_ln_b", "w_kv", "w_out", "ln2_g", "ln2_b",
    "ffn2_w_gate", "ffn2_w_up", "ffn2_w_down", "ln3_g", "ln3_b")


def kernel(x_prompt, x_sample, mem_prompt, mem_sample, ffn1_w_gate, ffn1_w_up, ffn1_w_down, ln1_g, ln1_b, w_in, pool_w, pool_scale, conv_dw_w, conv_dw_b, conv_ln_g, conv_ln_b, mem_ln_g, mem_ln_b, w_kv, w_out, ln2_g, ln2_b, ffn2_w_gate, ffn2_w_up, ffn2_w_down, ln3_g, ln3_b):
    stacked = dict(zip(_PARAM_NAMES, (
        ffn1_w_gate, ffn1_w_up, ffn1_w_down, ln1_g, ln1_b,
        w_in, pool_w, pool_scale, conv_dw_w, conv_dw_b, conv_ln_g, conv_ln_b,
        mem_ln_g, mem_ln_b, w_kv, w_out, ln2_g, ln2_b,
        ffn2_w_gate, ffn2_w_up, ffn2_w_down, ln3_g, ln3_b)))
    depth = ffn1_w_gate.shape[0]
    alpha = (2.0 * depth) ** 0.25
    layers = [_prepare_layer({k: v[layer] for k, v in stacked.items()}) for layer in range(depth)]

    def trunk(x, mem):
        for p in layers:
            x = _encoder_layer(x, mem, p, alpha)
        return x

    return (trunk(x_prompt, mem_prompt), trunk(x_sample, mem_sample))
```

```python
import functools
import math

import jax
import jax.numpy as jnp
from jax import lax
from jax.experimental import pallas as pl
from jax.experimental.pallas import tpu as pltpu

LN_EPS = 1e-5
MACARON_WEIGHT = 0.5
POOL_WINDOWS = (2, 4, 8, 16)
N_XATTN_HEADS = 4

LANES = 128
SUBLANES = 8
MXU_WIDTH = 256
HALO = 16
ROW_TILE = 512
FF_CHUNK = 256
MIX_ROWS = 64
NORM_ROWS = 128
PIN_LAG = 4
VMEM_LIMIT_BYTES = 56 * 1024 * 1024

_BF16 = jnp.bfloat16
_F32 = jnp.float32


def _layer_norm(r, g, b):
    mu = jnp.mean(r, axis=-1, keepdims=True)
    rc = r - mu
    var = jnp.mean(rc * rc, axis=-1, keepdims=True)
    return rc * lax.rsqrt(var + LN_EPS) * g + b


def _zero_tile(values):
    word = None
    for v in values:
        bits = pltpu.bitcast(v, jnp.uint32)
        for r in range(0, bits.shape[0], SUBLANES):
            for c in range(0, bits.shape[1], LANES):
                blk = bits[r:r + SUBLANES, c:c + LANES]
                word = blk if word is None else word | blk
    sixteen = jnp.uint32(16)
    word = lax.shift_right_logical(lax.shift_right_logical(word, sixteen), sixteen)
    return pltpu.bitcast(word, _F32)


def _add_into_tile(ref, index, zero):
    tile = ref[index].astype(_F32)
    zero = jnp.concatenate([zero] * (tile.shape[0] // SUBLANES), axis=0)
    ref[index] = (tile + zero).astype(ref.dtype)


def _first_tile(ref):
    rows = SUBLANES * 4 // ref.dtype.itemsize
    return (slice(0, rows), slice(0, LANES))


def _zero_fill(ref):
    rows = 8 * SUBLANES

    def body(k, carry):
        start = pl.multiple_of(k * rows, rows)
        ref[pl.ds(start, rows), :] = jnp.zeros((rows, ref.shape[1]), ref.dtype)
        return carry

    lax.fori_loop(0, ref.shape[0] // rows, body, 0)


class _Filler:
    def __init__(self, pieces, total_weight):
        self._pieces = list(pieces)
        self._total_cost = sum(c for c, _ in self._pieces)
        self._total_weight = total_weight
        self._weight = 0.0
        self._cost = 0.0

    def emit(self, weight):
        self._weight += weight
        target = self._total_cost * min(self._weight / self._total_weight, 1.0)
        produced = []
        while self._pieces and self._cost + 0.5 * self._pieces[0][0] <= target:
            cost, thunk = self._pieces.pop(0)
            value = thunk()
            if value is not None:
                produced.append(value)
            self._cost += cost
        return produced

    def done(self):
        return not self._pieces


def _swiglu_residual(x, wg_ref, wu_ref, wd_ref, h_scr, alpha, after_first_chunk=None):
    xb = x.astype(_BF16)
    d_ff = wg_ref.shape[1]
    for c0 in range(0, d_ff, FF_CHUNK):
        cols = slice(c0, c0 + FF_CHUNK)
        hg = jnp.dot(xb, wg_ref[:, cols], preferred_element_type=_F32)
        hu = jnp.dot(xb, wu_ref[:, cols], preferred_element_type=_F32)
        h_scr[:, cols] = (jax.nn.silu(hg) * hu).astype(_BF16)
        if c0 == 0 and after_first_chunk is not None:
            after_first_chunk(hg)
    y = jnp.dot(h_scr[...], wd_ref[...], preferred_element_type=_F32)
    return alpha * x + MACARON_WEIGHT * y


_UP_CHUNK_WEIGHT = 1.0
_DOWN_CHUNK_WEIGHT = 1.375
_OUT_PROJ_WEIGHT = 2.5


def _ffn_steps(d_ff, d_model):
    return ([("up", c0) for c0 in range(0, d_ff, FF_CHUNK)]
            + [("down", n0) for n0 in range(0, d_model, MXU_WIDTH)])


def _filler_weight(d_ff, d_model):
    steps = _ffn_steps(d_ff, d_model)[:-PIN_LAG]
    return _OUT_PROJ_WEIGHT + sum(
        _UP_CHUNK_WEIGHT if kind == "up" else _DOWN_CHUNK_WEIGHT for kind, _ in steps)


def _swiglu_residual_filled(x, wg_ref, wu_ref, wd_ref, xb_scr, h_scr, alpha, filler):
    d_ff, d_model = wd_ref.shape
    steps = _ffn_steps(d_ff, d_model)
    pending = {}
    parts = []
    for s, (kind, off) in enumerate(steps):
        lhs_ref = xb_scr if kind == "up" else h_scr
        if pending.get(s):
            _add_into_tile(lhs_ref, _first_tile(lhs_ref), _zero_tile(pending.pop(s)))
        produced = []
        if kind == "up":
            cols = slice(off, off + FF_CHUNK)
            hg = jnp.dot(xb_scr[...], wg_ref[:, cols], preferred_element_type=_F32)
            hu = jnp.dot(xb_scr[...], wu_ref[:, cols], preferred_element_type=_F32)
            h = jax.nn.silu(hg) * hu
            h_scr[:, cols] = h.astype(_BF16)
            produced.append(h)
        else:
            parts.append(jnp.dot(h_scr[...], wd_ref[:, off:off + MXU_WIDTH],
                                 preferred_element_type=_F32))
        if s + PIN_LAG < len(steps):
            produced += filler.emit(_UP_CHUNK_WEIGHT if kind == "up" else _DOWN_CHUNK_WEIGHT)
            pending.setdefault(s + PIN_LAG, []).extend(produced)
    y = jnp.concatenate(parts, axis=-1)
    return alpha * x + MACARON_WEIGHT * y


def _kv_kernel(mem_ref, g_ref, b_ref, wkt_ref, wv_ref, kt_ref, vb_ref, *, scale):
    memn = _layer_norm(mem_ref[0], g_ref[...], b_ref[...]).astype(_BF16)
    kt = lax.dot_general(wkt_ref[...], memn, (((1,), (1,)), ((), ())),
                         preferred_element_type=_F32) * scale
    v = jnp.dot(memn, wv_ref[...], preferred_element_type=_F32)
    n_mem, width = v.shape
    head_dim = width // N_XATTN_HEADS
    feat_row = lax.broadcasted_iota(jnp.int32, (width, n_mem), 0)
    feat_col = lax.broadcasted_iota(jnp.int32, (n_mem, width), 1)
    for h in range(N_XATTN_HEADS):
        lo, hi = h * head_dim, (h + 1) * head_dim
        in_row = (feat_row >= lo) & (feat_row < hi)
        in_col = (feat_col >= lo) & (feat_col < hi)
        kt_ref[0, :, h * n_mem:(h + 1) * n_mem] = jnp.where(in_row, kt, 0.0).astype(_BF16)
        vb_ref[0, h * n_mem:(h + 1) * n_mem, :] = jnp.where(in_col, v, 0.0).astype(_BF16)


def _ffn_in_kernel(x_ref, wg_ref, wu_ref, wd_ref, g_ref, b_ref, win_ref,
                   x1_ref, zc_ref, q_ref, r_scr, h_scr, *, alpha, pool_width, conv_width):
    @pl.when(pl.program_id(0) == 0)
    def _():
        _zero_fill(r_scr)

    def previous_tile(first_result):
        zero_row = jnp.concatenate(
            [_zero_tile([first_result[0:SUBLANES, 0:LANES]])[0:1, :]] * (g_ref.shape[1] // LANES),
            axis=1)
        x1 = _layer_norm(r_scr[...], g_ref[...] + zero_row, b_ref[...])
        x1_ref[0] = x1
        z = jnp.dot(x1.astype(_BF16), win_ref[...], preferred_element_type=_F32)
        pw, cw = pool_width, conv_width
        zc_ref[0, :, 0:pw] = z[:, 0:pw]
        zc_ref[0, :, pw:] = z[:, pw:pw + cw] * jax.nn.sigmoid(z[:, pw + cw:pw + 2 * cw])
        q_ref[0] = z[:, pw + 2 * cw:].astype(_BF16)

    r = _swiglu_residual(x_ref[0], wg_ref, wu_ref, wd_ref, h_scr, alpha, previous_tile)
    r_scr[...] = r


def _pool_pieces(ubuf, m_scr, t0, seq_len, rows):
    n_groups = len(POOL_WINDOWS)
    group = ubuf.shape[0] * LANES // n_groups
    pieces = []
    for j in range(ubuf.shape[0]):
        groups = [g for g in range(n_groups)
                  if g * group < (j + 1) * LANES and (g + 1) * group > j * LANES]

        def piece(j=j, groups=groups, r0=0):
            lane = j * LANES + lax.broadcasted_iota(jnp.int32, (MIX_ROWS, LANES), 1)
            row = lax.broadcasted_iota(jnp.int32, (MIX_ROWS, LANES), 0)
            in_group = [lane < (g + 1) * group for g in groups[:-1]]
            u0 = ubuf[j, pl.ds(HALO + r0, MIX_ROWS), :]
            sums = {}
            acc = None
            lo_prev, hi_prev = 0, -1
            for g in range(groups[-1] + 1):
                w = POOL_WINDOWS[g]
                lo, hi = -(w // 2), w - 1 - w // 2
                for d in list(range(lo, lo_prev)) + list(range(hi_prev + 1, hi + 1)):
                    term = u0 if d == 0 else ubuf[j, pl.ds(HALO + r0 + d, MIX_ROWS), :]
                    acc = term if acc is None else acc + term
                lo_prev, hi_prev = lo, hi
                sums[g] = acc
            win_sum = sums[groups[-1]]
            left = jnp.full((MIX_ROWS, LANES), POOL_WINDOWS[groups[-1]] // 2, jnp.int32)
            for g, mask in reversed(list(zip(groups[:-1], in_group))):
                win_sum = jnp.where(mask, sums[g], win_sum)
                left = jnp.where(mask, POOL_WINDOWS[g] // 2, left)
            t = t0 + r0 + row
            count = jnp.minimum(t + left, seq_len) - jnp.maximum(t - left, 0)
            m = win_sum / count.astype(_F32) - u0
            m_scr[r0:r0 + MIX_ROWS, j * LANES:(j + 1) * LANES] = m.astype(_BF16)
            return m

        cost = (MIX_ROWS // SUBLANES) * (POOL_WINDOWS[groups[-1]] + 24)
        for r0 in range(0, rows, MIX_ROWS):
            pieces.append((cost, functools.partial(piece, r0=r0)))
    return pieces


def _conv_pieces(hbuf, dww_ref, dwb_ref, out_scr, rows):
    taps = dww_ref.shape[0]
    base = HALO - taps // 2

    def piece(j, r0):
        lanes = slice(j * LANES, (j + 1) * LANES)
        w = dww_ref[:, lanes]
        acc = jnp.broadcast_to(dwb_ref[:, lanes], (MIX_ROWS, LANES))
        for k in range(taps):
            acc = acc + hbuf[j, pl.ds(r0 + base + k, MIX_ROWS), :] * w[k:k + 1, :]
        out_scr[r0:r0 + MIX_ROWS, lanes] = acc

    cost = (MIX_ROWS // SUBLANES) * 2 * taps
    return {r0: [(cost, functools.partial(piece, j, r0)) for j in range(hbuf.shape[0])]
            for r0 in range(0, rows, MIX_ROWS)}


def _mixer_pieces(z_ref, zp_ref, zn_ref, q_ref, kt_ref, vb_ref, poolw_ref, pscale_ref, dww_ref, dwb_ref,
                  clng_ref, clnb_ref, ubuf, hbuf, conv_scr, m_scr, s_scr, p_scr, mix_scr,
                  i, n_tiles, seq_len):
    rows = z_ref.shape[1]
    pw = poolw_ref.shape[0]
    cw = dww_ref.shape[1]
    n_mem = vb_ref.shape[1] // N_XATTN_HEADS
    first = i == 0
    last = i == n_tiles - 1
    row_vregs = rows // SUBLANES
    pieces = []

    def halo_block(buf, j, c0):
        cols = slice(c0 + j * LANES, c0 + (j + 1) * LANES)
        buf[j, 0:HALO, :] = jnp.where(first, 0.0, zp_ref[0, :, cols])
        buf[j, HALO:HALO + rows, :] = z_ref[0, :, cols]
        buf[j, HALO + rows:, :] = jnp.where(last, 0.0, zn_ref[0, :, cols])

    def scores():
        s_scr[...] = jnp.dot(q_ref[0], kt_ref[0], preferred_element_type=_F32)

    def softmax(h, r0):
        sh = s_scr[r0:r0 + NORM_ROWS, h * n_mem:(h + 1) * n_mem]
        e = jnp.exp(sh - jnp.max(sh, axis=-1, keepdims=True))
        p = e / jnp.sum(e, axis=-1, keepdims=True)
        p_scr[r0:r0 + NORM_ROWS, h * n_mem:(h + 1) * n_mem] = p.astype(_BF16)
        return p

    def attn_out():
        y = jnp.dot(p_scr[...], vb_ref[0], preferred_element_type=_F32)
        mix_scr[:, pw + cw:] = y.astype(_BF16)
        return y

    def pool_out():
        y = jnp.dot(m_scr[...], poolw_ref[...], preferred_element_type=_F32)
        y = y * pscale_ref[...]
        mix_scr[:, 0:pw] = y.astype(_BF16)
        return y

    def conv_out(r0):
        c = conv_scr[r0:r0 + MIX_ROWS, :]
        y = jax.nn.silu(_layer_norm(c, clng_ref[...], clnb_ref[...]))
        mix_scr[r0:r0 + MIX_ROWS, pw:pw + cw] = y.astype(_BF16)
        return y

    pieces += [(row_vregs, functools.partial(halo_block, ubuf, j, 0)) for j in range(pw // LANES)]
    pieces.append((row_vregs * 2, scores))
    pieces += [(row_vregs, functools.partial(halo_block, hbuf, j, pw)) for j in range(cw // LANES)]
    pieces += _pool_pieces(ubuf, m_scr, i * rows, seq_len, rows)
    pieces.append((row_vregs * 4, pool_out))
    pieces += [((NORM_ROWS // SUBLANES) * (n_mem // LANES) * 8, functools.partial(softmax, h, r0))
               for h in range(N_XATTN_HEADS) for r0 in range(0, rows, NORM_ROWS)]
    pieces.append((row_vregs * 2, attn_out))
    for r0, conv_pieces in _conv_pieces(hbuf, dww_ref, dwb_ref, conv_scr, rows).items():
        pieces += conv_pieces
        pieces.append(((MIX_ROWS // SUBLANES) * (cw // LANES) * 16,
                       functools.partial(conv_out, r0)))
    return pieces


def _mix_ffn_kernel(x1_ref, z_ref, zp_ref, zn_ref, q_ref, kt_ref, vb_ref,
                    poolw_ref, pscale_ref, dww_ref, dwb_ref, clng_ref, clnb_ref,
                    wout_ref, ln2g_ref, ln2b_ref, wg_ref, wu_ref, wd_ref, ln3g_ref, ln3b_ref,
                    o_ref, ubuf, hbuf, conv_scr, m_scr, s_scr, p_scr, mix_scr, xb_scr, h_scr, r_scr,
                    *, alpha, seq_len, n_tiles, n_total):
    t = pl.program_id(0)

    @pl.when(t == 0)
    def _():
        _zero_fill(mix_scr)
        _zero_fill(r_scr)

    def final_ln(r0):
        y = _layer_norm(r_scr[r0:r0 + MIX_ROWS, :], ln3g_ref[...], ln3b_ref[...])
        o_ref[0, r0:r0 + MIX_ROWS, :] = y
        return y

    rows = z_ref.shape[1]
    i = lax.rem(jnp.minimum(t, n_total - 1), n_tiles)
    d_ff, d_model = wd_ref.shape
    ln_cost = (MIX_ROWS // SUBLANES) * (d_model // LANES) * 10
    filler = _Filler(
        [(ln_cost, functools.partial(final_ln, r0)) for r0 in range(0, rows, MIX_ROWS)]
        + _mixer_pieces(z_ref, zp_ref, zn_ref, q_ref, kt_ref, vb_ref, poolw_ref, pscale_ref, dww_ref,
                        dwb_ref, clng_ref, clnb_ref, ubuf, hbuf, conv_scr, m_scr, s_scr, p_scr,
                        mix_scr, i, n_tiles, seq_len),
        total_weight=_filler_weight(d_ff, d_model))

    mix = jnp.dot(mix_scr[...], wout_ref[...], preferred_element_type=_F32)
    head = filler.emit(_OUT_PROJ_WEIGHT)
    alpha_anchored = alpha + _zero_tile(head)[0:1, 0:1]
    x2 = _layer_norm(alpha_anchored * x1_ref[0] + mix, ln2g_ref[...], ln2b_ref[...])
    xb_scr[...] = x2.astype(_BF16)
    r = _swiglu_residual_filled(x2, wg_ref, wu_ref, wd_ref, xb_scr, h_scr, alpha, filler)
    assert filler.done()
    r_scr[...] = r


def _const_spec(shape):
    zeros = (0,) * len(shape)
    return pl.BlockSpec(shape, lambda *_: zeros, pipeline_mode=pl.Buffered(1))


def _compiler_params(n_axes):
    return pltpu.CompilerParams(dimension_semantics=("arbitrary",) * n_axes,
                                vmem_limit_bytes=VMEM_LIMIT_BYTES)


def _row(v):
    return v.reshape(1, -1)


def _prepare_layer(p):
    bf = lambda w: w.astype(_BF16)
    aw = p["w_kv"].shape[1] // 2
    pool_w = p["pool_w"]
    q = {k: bf(p[k]) for k in ("ffn1_w_gate", "ffn1_w_up", "ffn1_w_down", "w_in", "w_out",
                               "ffn2_w_gate", "ffn2_w_up", "ffn2_w_down")}
    q.update({k: _row(p[k]) for k in ("ln1_g", "ln1_b", "pool_scale", "conv_dw_b", "conv_ln_g",
                                      "conv_ln_b", "mem_ln_g", "mem_ln_b", "ln2_g", "ln2_b",
                                      "ln3_g", "ln3_b")})
    q["conv_dw_w"] = p["conv_dw_w"]
    q["w_k_t"] = bf(p["w_kv"][:, :aw].T)
    q["w_v"] = bf(p["w_kv"][:, aw:])
    q["pool_w_bd"] = bf(jax.scipy.linalg.block_diag(*[pool_w[g] for g in range(pool_w.shape[0])]))
    return q


def _encoder_layer(x, mem, p, alpha):
    batch, seq_len, d_model = x.shape
    n_mem = mem.shape[1]
    d_ff = p["ffn1_w_gate"].shape[1]
    in_width = p["w_in"].shape[1]
    pw = p["pool_scale"].shape[1]
    cw = p["conv_dw_w"].shape[1]
    aw = p["w_v"].shape[1]
    rows = ROW_TILE
    assert seq_len % rows == 0 and rows % HALO == 0
    assert rows % MIX_ROWS == 0 and rows % NORM_ROWS == 0
    assert d_ff % FF_CHUNK == 0 and d_model % MXU_WIDTH == 0
    assert pw % LANES == 0 and cw % LANES == 0
    assert n_mem % LANES == 0 and aw % N_XATTN_HEADS == 0
    assert in_width == pw + 2 * cw + aw and pw + cw + aw == p["w_out"].shape[0]
    assert p["conv_dw_w"].shape[0] // 2 < HALO and POOL_WINDOWS[-1] // 2 <= HALO
    n_tiles = seq_len // rows
    n_total = batch * n_tiles
    halo_per_tile = rows // HALO
    n_halo_blocks = seq_len // HALO

    kt, vb = pl.pallas_call(
        functools.partial(_kv_kernel, scale=1.0 / math.sqrt(aw // N_XATTN_HEADS)),
        grid=(batch,),
        in_specs=[pl.BlockSpec((1, n_mem, d_model), lambda b: (b, 0, 0)),
                  _const_spec((1, d_model)), _const_spec((1, d_model)),
                  _const_spec((aw, d_model)), _const_spec((d_model, aw))],
        out_specs=[pl.BlockSpec((1, aw, N_XATTN_HEADS * n_mem), lambda b: (b, 0, 0)),
                   pl.BlockSpec((1, N_XATTN_HEADS * n_mem, aw), lambda b: (b, 0, 0))],
        out_shape=[jax.ShapeDtypeStruct((batch, aw, N_XATTN_HEADS * n_mem), _BF16),
                   jax.ShapeDtypeStruct((batch, N_XATTN_HEADS * n_mem, aw), _BF16)],
        compiler_params=_compiler_params(1),
        name="kv",
    )(mem, p["mem_ln_g"], p["mem_ln_b"], p["w_k_t"], p["w_v"])

    def staged_tile(t, lag):
        tile_id = jnp.clip(t - lag, 0, n_total - 1)
        return tile_id // n_tiles, tile_id % n_tiles

    def staged(width, lag):
        return pl.BlockSpec((1, rows, width), lambda t: (*staged_tile(t, lag), 0))

    zc_width = pw + cw
    x1, zc, q = pl.pallas_call(
        functools.partial(_ffn_in_kernel, alpha=alpha, pool_width=pw, conv_width=cw),
        grid=(n_total + 1,),
        in_specs=[staged(d_model, 0),
                  _const_spec((d_model, d_ff)), _const_spec((d_model, d_ff)),
                  _const_spec((d_ff, d_model)),
                  _const_spec((1, d_model)), _const_spec((1, d_model)),
                  _const_spec((d_model, in_width))],
        out_specs=[staged(d_model, 1), staged(zc_width, 1), staged(aw, 1)],
        out_shape=[jax.ShapeDtypeStruct((batch, seq_len, d_model), _F32),
                   jax.ShapeDtypeStruct((batch, seq_len, zc_width), _F32),
                   jax.ShapeDtypeStruct((batch, seq_len, aw), _BF16)],
        scratch_shapes=[pltpu.VMEM((rows, d_model), _F32),
                        pltpu.VMEM((rows, d_ff), _BF16)],
        compiler_params=_compiler_params(1),
        name="ffn_in",
    )(x, p["ffn1_w_gate"], p["ffn1_w_up"], p["ffn1_w_down"], p["ln1_g"], p["ln1_b"], p["w_in"])

    def halo_prev_map(t):
        b, i = staged_tile(t, 0)
        return b, jnp.maximum(i * halo_per_tile - 1, 0), 0

    def halo_next_map(t):
        b, i = staged_tile(t, 0)
        return b, jnp.minimum((i + 1) * halo_per_tile, n_halo_blocks - 1), 0

    per_batch = lambda s0, s1: pl.BlockSpec((1, s0, s1), lambda t: (staged_tile(t, 0)[0], 0, 0))
    return pl.pallas_call(
        functools.partial(_mix_ffn_kernel, alpha=alpha, seq_len=seq_len,
                          n_tiles=n_tiles, n_total=n_total),
        grid=(n_total + 2,),
        in_specs=[staged(d_model, 1),
                  staged(zc_width, 0),
                  pl.BlockSpec((1, HALO, zc_width), halo_prev_map),
                  pl.BlockSpec((1, HALO, zc_width), halo_next_map),
                  staged(aw, 0),
                  per_batch(aw, N_XATTN_HEADS * n_mem), per_batch(N_XATTN_HEADS * n_mem, aw),
                  _const_spec((pw, pw)), _const_spec((1, pw)),
                  _const_spec(p["conv_dw_w"].shape), _const_spec((1, cw)),
                  _const_spec((1, cw)), _const_spec((1, cw)),
                  _const_spec((pw + cw + aw, d_model)),
                  _const_spec((1, d_model)), _const_spec((1, d_model)),
                  _const_spec((d_model, d_ff)), _const_spec((d_model, d_ff)),
                  _const_spec((d_ff, d_model)),
                  _const_spec((1, d_model)), _const_spec((1, d_model))],
        out_specs=staged(d_model, 2),
        out_shape=jax.ShapeDtypeStruct((batch, seq_len, d_model), _F32),
        scratch_shapes=[pltpu.VMEM((pw // LANES, rows + 2 * HALO, LANES), _F32),
                        pltpu.VMEM((cw // LANES, rows + 2 * HALO, LANES), _F32),
                        pltpu.VMEM((rows, cw), _F32),
                        pltpu.VMEM((rows, pw), _BF16),
                        pltpu.VMEM((rows, N_XATTN_HEADS * n_mem), _F32),
                        pltpu.VMEM((rows, N_XATTN_HEADS * n_mem), _BF16),
                        pltpu.VMEM((rows, pw + cw + aw), _BF16),
                        pltpu.VMEM((rows, d_model), _BF16),
                        pltpu.VMEM((rows, d_ff), _BF16),
                        pltpu.VMEM((rows, d_model), _F32)],
        compiler_params=_compiler_params(1),
        name="mix",
    )(x1, zc, zc, zc, q, kt, vb,
      p["pool_w_bd"], p["pool_scale"], p["conv_dw_w"], p["conv_dw_b"],
      p["conv_ln_g"], p["conv_ln_b"], p["w_out"], p["ln2_g"], p["ln2_b"],
      p["ffn2_w_gate"], p["ffn2_w_up"], p["ffn2_w_down"], p["ln3_g"], p["ln3_b"])


_PARAM_NAMES = (
    "ffn1_w_gate", "ffn1_w_up", "ffn1_w_down", "ln1_g", "ln1_b",
    "w_in", "pool_w", "pool_scale", "conv_dw_w", "conv_dw_b", "conv_ln_g", "conv_ln_b",
    "mem_ln_g", "mem_ln_b", "w_kv", "w_out", "ln2_g", "ln2_b",
    "ffn2_w_gate", "ffn2_w_up", "ffn2_w_down", "ln3_g", "ln3_b")


def kernel(x_prompt, x_sample, mem_prompt, mem_sample, ffn1_w_gate, ffn1_w_up, ffn1_w_down, ln1_g, ln1_b, w_in, pool_w, pool_scale, conv_dw_w, conv_dw_b, conv_ln_g, conv_ln_b, mem_ln_g, mem_ln_b, w_kv, w_out, ln2_g, ln2_b, ffn2_w_gate, ffn2_w_up, ffn2_w_down, ln3_g, ln3_b):
    stacked = dict(zip(_PARAM_NAMES, (
        ffn1_w_gate, ffn1_w_up, ffn1_w_down, ln1_g, ln1_b,
        w_in, pool_w, pool_scale, conv_dw_w, conv_dw_b, conv_ln_g, conv_ln_b,
        mem_ln_g, mem_ln_b, w_kv, w_out, ln2_g, ln2_b,
        ffn2_w_gate, ffn2_w_up, ffn2_w_down, ln3_g, ln3_b)))
    depth = ffn1_w_gate.shape[0]
    alpha = (2.0 * depth) ** 0.25
    layers = [_prepare_layer({k: v[layer] for k, v in stacked.items()}) for layer in range(depth)]

    def trunk(x, mem):
        for p in layers:
            x = _encoder_layer(x, mem, p, alpha)
        return x

    return (trunk(x_prompt, mem_prompt), trunk(x_sample, mem_sample))
```

```python
import functools
import math

import jax
import jax.numpy as jnp
from jax import lax
from jax.experimental import pallas as pl
from jax.experimental.pallas import tpu as pltpu

LN_EPS = 1e-5
MACARON_WEIGHT = 0.5
POOL_WINDOWS = (2, 4, 8, 16)
N_XATTN_HEADS = 4

LANES = 128
SUBLANES = 8
MXU_WIDTH = 256
HALO = 16
ROW_TILE = 512
FF_CHUNK = 256
MIX_ROWS = 64
NORM_ROWS = 128
PIN_LAG = 4
VMEM_LIMIT_BYTES = 56 * 1024 * 1024

_BF16 = jnp.bfloat16
_F32 = jnp.float32


def _layer_norm(r, g, b):
    mu = jnp.mean(r, axis=-1, keepdims=True)
    rc = r - mu
    var = jnp.mean(rc * rc, axis=-1, keepdims=True)
    return rc * lax.rsqrt(var + LN_EPS) * g + b


def _zero_tile(values):
    word = None
    for v in values:
        bits = pltpu.bitcast(v, jnp.uint32)
        for r in range(0, bits.shape[0], SUBLANES):
            for c in range(0, bits.shape[1], LANES):
                blk = bits[r:r + SUBLANES, c:c + LANES]
                word = blk if word is None else word | blk
    sixteen = jnp.uint32(16)
    word = lax.shift_right_logical(lax.shift_right_logical(word, sixteen), sixteen)
    return pltpu.bitcast(word, _F32)


def _add_into_tile(ref, index, zero):
    tile = ref[index].astype(_F32)
    zero = jnp.concatenate([zero] * (tile.shape[0] // SUBLANES), axis=0)
    ref[index] = (tile + zero).astype(ref.dtype)


def _first_tile(ref):
    rows = SUBLANES * 4 // ref.dtype.itemsize
    return (slice(0, rows), slice(0, LANES))


def _zero_fill(ref):
    rows = 8 * SUBLANES

    def body(k, carry):
        start = pl.multiple_of(k * rows, rows)
        ref[pl.ds(start, rows), :] = jnp.zeros((rows, ref.shape[1]), ref.dtype)
        return carry

    lax.fori_loop(0, ref.shape[0] // rows, body, 0)


class _Filler:
    def __init__(self, pieces, total_weight):
        self._pieces = list(pieces)
        self._total_cost = sum(c for c, _ in self._pieces)
        self._total_weight = total_weight
        self._weight = 0.0
        self._cost = 0.0

    def emit(self, weight):
        self._weight += weight
        target = self._total_cost * min(self._weight / self._total_weight, 1.0)
        produced = []
        while self._pieces and self._cost + 0.5 * self._pieces[0][0] <= target:
            cost, thunk = self._pieces.pop(0)
            value = thunk()
            if value is not None:
                produced.append(value)
            self._cost += cost
        return produced

    def done(self):
        return not self._pieces


def _swiglu_residual(x, wg_ref, wu_ref, wd_ref, h_scr, alpha, after_first_chunk=None):
    xb = x.astype(_BF16)
    d_ff = wg_ref.shape[1]
    for c0 in range(0, d_ff, FF_CHUNK):
        cols = slice(c0, c0 + FF_CHUNK)
        hg = jnp.dot(xb, wg_ref[:, cols], preferred_element_type=_F32)
        hu = jnp.dot(xb, wu_ref[:, cols], preferred_element_type=_F32)
        h_scr[:, cols] = (jax.nn.silu(hg) * hu).astype(_BF16)
        if c0 == 0 and after_first_chunk is not None:
            after_first_chunk(hg)
    y = jnp.dot(h_scr[...], wd_ref[...], preferred_element_type=_F32)
    return alpha * x + MACARON_WEIGHT * y


_UP_CHUNK_WEIGHT = 1.0
_DOWN_CHUNK_WEIGHT = 1.375
_OUT_PROJ_WEIGHT = 2.5


def _ffn_steps(d_ff, d_model):
    return ([("up", c0) for c0 in range(0, d_ff, FF_CHUNK)]
            + [("down", n0) for n0 in range(0, d_model, MXU_WIDTH)])


def _filler_weight(d_ff, d_model):
    steps = _ffn_steps(d_ff, d_model)[:-PIN_LAG]
    return _OUT_PROJ_WEIGHT + sum(
        _UP_CHUNK_WEIGHT if kind == "up" else _DOWN_CHUNK_WEIGHT for kind, _ in steps)


def _swiglu_residual_filled(x, wg_ref, wu_ref, wd_ref, xb_scr, h_scr, alpha, filler):
    d_ff, d_model = wd_ref.shape
    steps = _ffn_steps(d_ff, d_model)
    pending = {}
    parts = []
    for s, (kind, off) in enumerate(steps):
        lhs_ref = xb_scr if kind == "up" else h_scr
        if pending.get(s):
            _add_into_tile(lhs_ref, _first_tile(lhs_ref), _zero_tile(pending.pop(s)))
        produced = []
        if kind == "up":
            cols = slice(off, off + FF_CHUNK)
            hg = jnp.dot(xb_scr[...], wg_ref[:, cols], preferred_element_type=_F32)
            hu = jnp.dot(xb_scr[...], wu_ref[:, cols], preferred_element_type=_F32)
            h_scr[:, cols] = (jax.nn.silu(hg) * hu).astype(_BF16)
        else:
            parts.append(jnp.dot(h_scr[...], wd_ref[:, off:off + MXU_WIDTH],
                                 preferred_element_type=_F32))
        if s + PIN_LAG < len(steps):
            produced += filler.emit(_UP_CHUNK_WEIGHT if kind == "up" else _DOWN_CHUNK_WEIGHT)
            pending.setdefault(s + PIN_LAG, []).extend(produced)
    y = jnp.concatenate(parts, axis=-1)
    return alpha * x + MACARON_WEIGHT * y


def _kv_kernel(mem_ref, g_ref, b_ref, wkt_ref, wv_ref, kt_ref, vb_ref, *, scale):
    memn = _layer_norm(mem_ref[0], g_ref[...], b_ref[...]).astype(_BF16)
    kt = lax.dot_general(wkt_ref[...], memn, (((1,), (1,)), ((), ())),
                         preferred_element_type=_F32) * scale
    v = jnp.dot(memn, wv_ref[...], preferred_element_type=_F32)
    n_mem, width = v.shape
    head_dim = width // N_XATTN_HEADS
    feat_row = lax.broadcasted_iota(jnp.int32, (width, n_mem), 0)
    feat_col = lax.broadcasted_iota(jnp.int32, (n_mem, width), 1)
    for h in range(N_XATTN_HEADS):
        lo, hi = h * head_dim, (h + 1) * head_dim
        in_row = (feat_row >= lo) & (feat_row < hi)
        in_col = (feat_col >= lo) & (feat_col < hi)
        kt_ref[0, :, h * n_mem:(h + 1) * n_mem] = jnp.where(in_row, kt, 0.0).astype(_BF16)
        vb_ref[0, h * n_mem:(h + 1) * n_mem, :] = jnp.where(in_col, v, 0.0).astype(_BF16)


def _ffn_in_kernel(x_ref, wg_ref, wu_ref, wd_ref, g_ref, b_ref, win_ref,
                   x1_ref, zc_ref, q_ref, r_scr, h_scr, *, alpha, pool_width, conv_width):
    @pl.when(pl.program_id(0) == 0)
    def _():
        _zero_fill(r_scr)

    def previous_tile(first_result):
        zero_row = jnp.concatenate(
            [_zero_tile([first_result[0:SUBLANES, 0:LANES]])[0:1, :]] * (g_ref.shape[1] // LANES),
            axis=1)
        x1 = _layer_norm(r_scr[...], g_ref[...] + zero_row, b_ref[...])
        x1_ref[0] = x1
        z = jnp.dot(x1.astype(_BF16), win_ref[...], preferred_element_type=_F32)
        pw, cw = pool_width, conv_width
        zc_ref[0, :, 0:pw] = z[:, 0:pw]
        zc_ref[0, :, pw:] = z[:, pw:pw + cw] * jax.nn.sigmoid(z[:, pw + cw:pw + 2 * cw])
        q_ref[0] = z[:, pw + 2 * cw:].astype(_BF16)

    r = _swiglu_residual(x_ref[0], wg_ref, wu_ref, wd_ref, h_scr, alpha, previous_tile)
    r_scr[...] = r


def _pool_pieces(ubuf, m_scr, t0, seq_len, rows):
    n_groups = len(POOL_WINDOWS)
    group = ubuf.shape[0] * LANES // n_groups
    pieces = []
    for j in range(ubuf.shape[0]):
        groups = [g for g in range(n_groups)
                  if g * group < (j + 1) * LANES and (g + 1) * group > j * LANES]

        def piece(j=j, groups=groups, r0=0):
            lane = j * LANES + lax.broadcasted_iota(jnp.int32, (MIX_ROWS, LANES), 1)
            row = lax.broadcasted_iota(jnp.int32, (MIX_ROWS, LANES), 0)
            in_group = [lane < (g + 1) * group for g in groups[:-1]]
            u0 = ubuf[j, pl.ds(HALO + r0, MIX_ROWS), :]
            sums = {}
            acc = None
            lo_prev, hi_prev = 0, -1
            for g in range(groups[-1] + 1):
                w = POOL_WINDOWS[g]
                lo, hi = -(w // 2), w - 1 - w // 2
                for d in list(range(lo, lo_prev)) + list(range(hi_prev + 1, hi + 1)):
                    term = u0 if d == 0 else ubuf[j, pl.ds(HALO + r0 + d, MIX_ROWS), :]
                    acc = term if acc is None else acc + term
                lo_prev, hi_prev = lo, hi
                sums[g] = acc
            win_sum = sums[groups[-1]]
            left = jnp.full((MIX_ROWS, LANES), POOL_WINDOWS[groups[-1]] // 2, jnp.int32)
            for g, mask in reversed(list(zip(groups[:-1], in_group))):
                win_sum = jnp.where(mask, sums[g], win_sum)
                left = jnp.where(mask, POOL_WINDOWS[g] // 2, left)
            t = t0 + r0 + row
            count = jnp.minimum(t + left, seq_len) - jnp.maximum(t - left, 0)
            m = win_sum / count.astype(_F32) - u0
            m_scr[r0:r0 + MIX_ROWS, j * LANES:(j + 1) * LANES] = m.astype(_BF16)
            return m

        cost = (MIX_ROWS // SUBLANES) * (POOL_WINDOWS[groups[-1]] + 24)
        for r0 in range(0, rows, MIX_ROWS):
            pieces.append((cost, functools.partial(piece, r0=r0)))
    return pieces


def _conv_pieces(hbuf, dww_ref, dwb_ref, out_scr, rows):
    taps = dww_ref.shape[0]
    base = HALO - taps // 2

    def piece(j, r0):
        lanes = slice(j * LANES, (j + 1) * LANES)
        w = dww_ref[:, lanes]
        acc = jnp.broadcast_to(dwb_ref[:, lanes], (MIX_ROWS, LANES))
        for k in range(taps):
            acc = acc + hbuf[j, pl.ds(r0 + base + k, MIX_ROWS), :] * w[k:k + 1, :]
        out_scr[r0:r0 + MIX_ROWS, lanes] = acc

    cost = (MIX_ROWS // SUBLANES) * 2 * taps
    return {r0: [(cost, functools.partial(piece, j, r0)) for j in range(hbuf.shape[0])]
            for r0 in range(0, rows, MIX_ROWS)}


def _mixer_pieces(z_ref, zp_ref, zn_ref, q_ref, kt_ref, vb_ref, poolw_ref, pscale_ref, dww_ref, dwb_ref,
                  clng_ref, clnb_ref, ubuf, hbuf, conv_scr, m_scr, s_scr, p_scr, mix_scr,
                  i, n_tiles, seq_len):
    rows = z_ref.shape[1]
    pw = poolw_ref.shape[0]
    cw = dww_ref.shape[1]
    n_mem = vb_ref.shape[1] // N_XATTN_HEADS
    first = i == 0
    last = i == n_tiles - 1
    row_vregs = rows // SUBLANES
    pieces = []

    def halo_block(buf, j, c0):
        cols = slice(c0 + j * LANES, c0 + (j + 1) * LANES)
        buf[j, 0:HALO, :] = jnp.where(first, 0.0, zp_ref[0, :, cols])
        buf[j, HALO:HALO + rows, :] = z_ref[0, :, cols]
        buf[j, HALO + rows:, :] = jnp.where(last, 0.0, zn_ref[0, :, cols])

    def scores():
        s_scr[...] = jnp.dot(q_ref[0], kt_ref[0], preferred_element_type=_F32)

    def softmax(h, r0):
        sh = s_scr[r0:r0 + NORM_ROWS, h * n_mem:(h + 1) * n_mem]
        e = jnp.exp(sh - jnp.max(sh, axis=-1, keepdims=True))
        p = e / jnp.sum(e, axis=-1, keepdims=True)
        p_scr[r0:r0 + NORM_ROWS, h * n_mem:(h + 1) * n_mem] = p.astype(_BF16)
        return p

    def attn_out():
        y = jnp.dot(p_scr[...], vb_ref[0], preferred_element_type=_F32)
        mix_scr[:, pw + cw:] = y.astype(_BF16)
        return y

    def pool_out():
        y = jnp.dot(m_scr[...], poolw_ref[...], preferred_element_type=_F32)
        y = y * pscale_ref[...]
        mix_scr[:, 0:pw] = y.astype(_BF16)
        return y

    def conv_out(r0):
        c = conv_scr[r0:r0 + MIX_ROWS, :]
        y = jax.nn.silu(_layer_norm(c, clng_ref[...], clnb_ref[...]))
        mix_scr[r0:r0 + MIX_ROWS, pw:pw + cw] = y.astype(_BF16)
        return y

    pieces += [(row_vregs, functools.partial(halo_block, ubuf, j, 0)) for j in range(pw // LANES)]
    pieces.append((row_vregs * 2, scores))
    pieces += [(row_vregs, functools.partial(halo_block, hbuf, j, pw)) for j in range(cw // LANES)]
    pieces += _pool_pieces(ubuf, m_scr, i * rows, seq_len, rows)
    pieces.append((row_vregs * 4, pool_out))
    pieces += [((NORM_ROWS // SUBLANES) * (n_mem // LANES) * 8, functools.partial(softmax, h, r0))
               for h in range(N_XATTN_HEADS) for r0 in range(0, rows, NORM_ROWS)]
    pieces.append((row_vregs * 2, attn_out))
    for r0, conv_pieces in _conv_pieces(hbuf, dww_ref, dwb_ref, conv_scr, rows).items():
        pieces += conv_pieces
        pieces.append(((MIX_ROWS // SUBLANES) * (cw // LANES) * 16,
                       functools.partial(conv_out, r0)))
    return pieces


def _mix_ffn_kernel(x1_ref, z_ref, zp_ref, zn_ref, q_ref, kt_ref, vb_ref,
                    poolw_ref, pscale_ref, dww_ref, dwb_ref, clng_ref, clnb_ref,
                    wout_ref, ln2g_ref, ln2b_ref, wg_ref, wu_ref, wd_ref, ln3g_ref, ln3b_ref,
                    o_ref, ubuf, hbuf, conv_scr, m_scr, s_scr, p_scr, mix_scr, xb_scr, h_scr, r_scr,
                    *, alpha, seq_len, n_tiles, n_total):
    t = pl.program_id(0)

    @pl.when(t == 0)
    def _():
        _zero_fill(mix_scr)
        _zero_fill(r_scr)

    def final_ln(r0):
        y = _layer_norm(r_scr[r0:r0 + MIX_ROWS, :], ln3g_ref[...], ln3b_ref[...])
        o_ref[0, r0:r0 + MIX_ROWS, :] = y
        return y

    rows = z_ref.shape[1]
    i = lax.rem(jnp.minimum(t, n_total - 1), n_tiles)
    d_ff, d_model = wd_ref.shape
    ln_cost = (MIX_ROWS // SUBLANES) * (d_model // LANES) * 10
    filler = _Filler(
        [(ln_cost, functools.partial(final_ln, r0)) for r0 in range(0, rows, MIX_ROWS)]
        + _mixer_pieces(z_ref, zp_ref, zn_ref, q_ref, kt_ref, vb_ref, poolw_ref, pscale_ref, dww_ref,
                        dwb_ref, clng_ref, clnb_ref, ubuf, hbuf, conv_scr, m_scr, s_scr, p_scr,
                        mix_scr, i, n_tiles, seq_len),
        total_weight=_filler_weight(d_ff, d_model))

    mix = jnp.dot(mix_scr[...], wout_ref[...], preferred_element_type=_F32)
    head = filler.emit(_OUT_PROJ_WEIGHT)
    alpha_anchored = alpha + _zero_tile(head)[0:1, 0:1]
    x2 = _layer_norm(alpha_anchored * x1_ref[0] + mix, ln2g_ref[...], ln2b_ref[...])
    xb_scr[...] = x2.astype(_BF16)
    r = _swiglu_residual_filled(x2, wg_ref, wu_ref, wd_ref, xb_scr, h_scr, alpha, filler)
    assert filler.done()
    r_scr[...] = r


def _const_spec(shape):
    zeros = (0,) * len(shape)
    return pl.BlockSpec(shape, lambda *_: zeros, pipeline_mode=pl.Buffered(1))


def _compiler_params(n_axes):
    return pltpu.CompilerParams(dimension_semantics=("arbitrary",) * n_axes,
                                vmem_limit_bytes=VMEM_LIMIT_BYTES)


def _row(v):
    return v.reshape(1, -1)


def _prepare_layer(p):
    bf = lambda w: w.astype(_BF16)
    aw = p["w_kv"].shape[1] // 2
    pool_w = p["pool_w"]
    q = {k: bf(p[k]) for k in ("ffn1_w_gate", "ffn1_w_up", "ffn1_w_down", "w_in", "w_out",
                               "ffn2_w_gate", "ffn2_w_up", "ffn2_w_down")}
    q.update({k: _row(p[k]) for k in ("ln1_g", "ln1_b", "pool_scale", "conv_dw_b", "conv_ln_g",
                                      "conv_ln_b", "mem_ln_g", "mem_ln_b", "ln2_g", "ln2_b",
                                      "ln3_g", "ln3_b")})
    q["conv_dw_w"] = p["conv_dw_w"]
    q["w_k_t"] = bf(p["w_kv"][:, :aw].T)
    q["w_v"] = bf(p["w_kv"][:, aw:])
    q["pool_w_bd"] = bf(jax.scipy.linalg.block_diag(*[pool_w[g] for g in range(pool_w.shape[0])]))
    return q


def _encoder_layer(x, mem, p, alpha):
    batch, seq_len, d_model = x.shape
    n_mem = mem.shape[1]
    d_ff = p["ffn1_w_gate"].shape[1]
    in_width = p["w_in"].shape[1]
    pw = p["pool_scale"].shape[1]
    cw = p["conv_dw_w"].shape[1]
    aw = p["w_v"].shape[1]
    rows = ROW_TILE
    assert seq_len % rows == 0 and rows % HALO == 0
    assert rows % MIX_ROWS == 0 and rows % NORM_ROWS == 0
    assert d_ff % FF_CHUNK == 0 and d_model % MXU_WIDTH == 0
    assert pw % LANES == 0 and cw % LANES == 0
    assert n_mem % LANES == 0 and aw % N_XATTN_HEADS == 0
    assert in_width == pw + 2 * cw + aw and pw + cw + aw == p["w_out"].shape[0]
    assert p["conv_dw_w"].shape[0] // 2 < HALO and POOL_WINDOWS[-1] // 2 <= HALO
    n_tiles = seq_len // rows
    n_total = batch * n_tiles
    halo_per_tile = rows // HALO
    n_halo_blocks = seq_len // HALO

    kt, vb = pl.pallas_call(
        functools.partial(_kv_kernel, scale=1.0 / math.sqrt(aw // N_XATTN_HEADS)),
        grid=(batch,),
        in_specs=[pl.BlockSpec((1, n_mem, d_model), lambda b: (b, 0, 0)),
                  _const_spec((1, d_model)), _const_spec((1, d_model)),
                  _const_spec((aw, d_model)), _const_spec((d_model, aw))],
        out_specs=[pl.BlockSpec((1, aw, N_XATTN_HEADS * n_mem), lambda b: (b, 0, 0)),
                   pl.BlockSpec((1, N_XATTN_HEADS * n_mem, aw), lambda b: (b, 0, 0))],
        out_shape=[jax.ShapeDtypeStruct((batch, aw, N_XATTN_HEADS * n_mem), _BF16),
                   jax.ShapeDtypeStruct((batch, N_XATTN_HEADS * n_mem, aw), _BF16)],
        compiler_params=_compiler_params(1),
        name="kv",
    )(mem, p["mem_ln_g"], p["mem_ln_b"], p["w_k_t"], p["w_v"])

    def staged_tile(t, lag):
        tile_id = jnp.clip(t - lag, 0, n_total - 1)
        return tile_id // n_tiles, tile_id % n_tiles

    def staged(width, lag):
        return pl.BlockSpec((1, rows, width), lambda t: (*staged_tile(t, lag), 0))

    zc_width = pw + cw
    x1, zc, q = pl.pallas_call(
        functools.partial(_ffn_in_kernel, alpha=alpha, pool_width=pw, conv_width=cw),
        grid=(n_total + 1,),
        in_specs=[staged(d_model, 0),
                  _const_spec((d_model, d_ff)), _const_spec((d_model, d_ff)),
                  _const_spec((d_ff, d_model)),
                  _const_spec((1, d_model)), _const_spec((1, d_model)),
                  _const_spec((d_model, in_width))],
        out_specs=[staged(d_model, 1), staged(zc_width, 1), staged(aw, 1)],
        out_shape=[jax.ShapeDtypeStruct((batch, seq_len, d_model), _F32),
                   jax.ShapeDtypeStruct((batch, seq_len, zc_width), _F32),
                   jax.ShapeDtypeStruct((batch, seq_len, aw), _BF16)],
        scratch_shapes=[pltpu.VMEM((rows, d_model), _F32),
                        pltpu.VMEM((rows, d_ff), _BF16)],
        compiler_params=_compiler_params(1),
        name="ffn_in",
    )(x, p["ffn1_w_gate"], p["ffn1_w_up"], p["ffn1_w_down"], p["ln1_g"], p["ln1_b"], p["w_in"])

    def halo_prev_map(t):
        b, i = staged_tile(t, 0)
        return b, jnp.maximum(i * halo_per_tile - 1, 0), 0

    def halo_next_map(t):
        b, i = staged_tile(t, 0)
        return b, jnp.minimum((i + 1) * halo_per_tile, n_halo_blocks - 1), 0

    per_batch = lambda s0, s1: pl.BlockSpec((1, s0, s1), lambda t: (staged_tile(t, 0)[0], 0, 0))
    return pl.pallas_call(
        functools.partial(_mix_ffn_kernel, alpha=alpha, seq_len=seq_len,
                          n_tiles=n_tiles, n_total=n_total),
        grid=(n_total + 2,),
        in_specs=[staged(d_model, 1),
                  staged(zc_width, 0),
                  pl.BlockSpec((1, HALO, zc_width), halo_prev_map),
                  pl.BlockSpec((1, HALO, zc_width), halo_next_map),
                  staged(aw, 0),
                  per_batch(aw, N_XATTN_HEADS * n_mem), per_batch(N_XATTN_HEADS * n_mem, aw),
                  _const_spec((pw, pw)), _const_spec((1, pw)),
                  _const_spec(p["conv_dw_w"].shape), _const_spec((1, cw)),
                  _const_spec((1, cw)), _const_spec((1, cw)),
                  _const_spec((pw + cw + aw, d_model)),
                  _const_spec((1, d_model)), _const_spec((1, d_model)),
                  _const_spec((d_model, d_ff)), _const_spec((d_model, d_ff)),
                  _const_spec((d_ff, d_model)),
                  _const_spec((1, d_model)), _const_spec((1, d_model))],
        out_specs=staged(d_model, 2),
        out_shape=jax.ShapeDtypeStruct((batch, seq_len, d_model), _F32),
        scratch_shapes=[pltpu.VMEM((pw // LANES, rows + 2 * HALO, LANES), _F32),
                        pltpu.VMEM((cw // LANES, rows + 2 * HALO, LANES), _F32),
                        pltpu.VMEM((rows, cw), _F32),
                        pltpu.VMEM((rows, pw), _BF16),
                        pltpu.VMEM((rows, N_XATTN_HEADS * n_mem), _F32),
                        pltpu.VMEM((rows, N_XATTN_HEADS * n_mem), _BF16),
                        pltpu.VMEM((rows, pw + cw + aw), _BF16),
                        pltpu.VMEM((rows, d_model), _BF16),
                        pltpu.VMEM((rows, d_ff), _BF16),
                        pltpu.VMEM((rows, d_model), _F32)],
        compiler_params=_compiler_params(1),
        name="mix",
    )(x1, zc, zc, zc, q, kt, vb,
      p["pool_w_bd"], p["pool_scale"], p["conv_dw_w"], p["conv_dw_b"],
      p["conv_ln_g"], p["conv_ln_b"], p["w_out"], p["ln2_g"], p["ln2_b"],
      p["ffn2_w_gate"], p["ffn2_w_up"], p["ffn2_w_down"], p["ln3_g"], p["ln3_b"])


_PARAM_NAMES = (
    "ffn1_w_gate", "ffn1_w_up", "ffn1_w_down", "ln1_g", "ln1_b",
    "w_in", "pool_w", "pool_scale", "conv_dw_w", "conv_dw_b", "conv_ln_g", "conv_ln_b",
    "mem_ln_g", "mem_ln_b", "w_kv", "w_out", "ln2_g", "ln2_b",
    "ffn2_w_gate", "ffn2_w_up", "ffn2_w_down", "ln3_g", "ln3_b")


def kernel(x_prompt, x_sample, mem_prompt, mem_sample, ffn1_w_gate, ffn1_w_up, ffn1_w_down, ln1_g, ln1_b, w_in, pool_w, pool_scale, conv_dw_w, conv_dw_b, conv_ln_g, conv_ln_b, mem_ln_g, mem_ln_b, w_kv, w_out, ln2_g, ln2_b, ffn2_w_gate, ffn2_w_up, ffn2_w_down, ln3_g, ln3_b):
    stacked = dict(zip(_PARAM_NAMES, (
        ffn1_w_gate, ffn1_w_up, ffn1_w_down, ln1_g, ln1_b,
        w_in, pool_w, pool_scale, conv_dw_w, conv_dw_b, conv_ln_g, conv_ln_b,
        mem_ln_g, mem_ln_b, w_kv, w_out, ln2_g, ln2_b,
        ffn2_w_gate, ffn2_w_up, ffn2_w_down, ln3_g, ln3_b)))
    depth = ffn1_w_gate.shape[0]
    alpha = (2.0 * depth) ** 0.25
    layers = [_prepare_layer({k: v[layer] for k, v in stacked.items()}) for layer in range(depth)]

    def trunk(x, mem):
        for p in layers:
            x = _encoder_layer(x, mem, p, alpha)
        return x

    return (trunk(x_prompt, mem_prompt), trunk(x_sample, mem_sample))
```
